```python
import jax, jax.numpy as jnp
from jax import lax
import numpy as np

D_MODEL = 1024
BATCH = 8
SEQ = 2048
DEPTH = 2
DEC_BATCH = 128
DEC_SEQ = 1
PAST_LEN = 16384
PAGE_SIZE = 128

MIX_WIDTH = D_MODEL
HG_HEADS = 4
HG_DV = (MIX_WIDTH // 2) // HG_HEADS
HG_DK = HG_DV
GLA_HEADS = 4
GLA_DV = (MIX_WIDTH // 2) // GLA_HEADS
GLA_DK = GLA_DV // 2
GLA_GATE_RANK = 16
GLA_GATE_NORM = 16.0
CHUNK = 64
N_EXPERTS = 32
TOP_K = 4
D_FF = D_MODEL
SWIGLU_LIMIT = 7.0
SWIGLU_ALPHA = 1.702
MOE_BLOCK = 128
EPS = 1e-6
PROJ_WIDTHS = (HG_HEADS * HG_DK, HG_HEADS * HG_DK, HG_HEADS * HG_DV, HG_HEADS * HG_DV,
               GLA_HEADS * GLA_DK, GLA_HEADS * GLA_DK, GLA_HEADS * GLA_DV, GLA_HEADS * GLA_DV,
               GLA_GATE_RANK)
D_IN_PROJ = sum(PROJ_WIDTHS)

kernel_name = "hymba_hgrn2_gla_moe_adaln_step"


def rms_norm(x, g):
    xf = x.astype(jnp.float32)
    y = xf * lax.rsqrt(jnp.mean(xf * xf, axis=-1, keepdims=True) + EPS)
    return (y * g.astype(jnp.float32)).astype(x.dtype)


def gated_recurrence(q, k, v, log_a, s0):
    B, L, H, DK = q.shape
    DV = v.shape[-1]
    C = min(CHUNK, L)
    n = -(-L // C)
    pad = n * C - L
    if pad:
        pw = ((0, 0), (0, pad), (0, 0), (0, 0))
        q, k, v, log_a = (jnp.pad(t, pw) for t in (q, k, v, log_a))

    def to_chunks(t):
        return t.reshape(B, n, C, H, t.shape[-1]).transpose(1, 0, 3, 2, 4)

    causal = jnp.tril(jnp.ones((C, C), dtype=bool))[:, :, None]

    def step(S, inp):
        qc, kc, vc, gc = inp
        b = jnp.cumsum(gc, axis=2)
        diff = b[:, :, :, None, :] - b[:, :, None, :, :]
        decay = jnp.exp(jnp.where(causal, diff, -jnp.inf))
        att = jnp.einsum('bhtd,bhsd,bhtsd->bhts', qc, kc, decay)
        o = jnp.einsum('bhts,bhsv->bhtv', att, vc) + jnp.einsum('bhtd,bhdv->bhtv', qc * jnp.exp(b), S)
        b_last = b[:, :, -1:, :]
        S_new = jnp.exp(b_last[:, :, 0, :])[..., None] * S + jnp.einsum(
            'bhsd,bhsv->bhdv', kc * jnp.exp(b_last - b), vc)
        return S_new, o

    S, o = lax.scan(step, s0, (to_chunks(q), to_chunks(k), to_chunks(v), to_chunks(log_a)))
    o = o.transpose(1, 0, 3, 2, 4).reshape(B, n * C, H, DV)[:, :L]
    return o, S


def token_mixers(h, s_hg, s_gl, lb, w_in, w_gk2, b_gk2, gn_hg, gn_gl, w_o):
    B, L, _ = h.shape
    f32 = jnp.float32
    proj = h @ w_in
    splits = np.cumsum(PROJ_WIDTHS)[:-1].tolist()
    hq, hf, hi, hg, gq, gk, gv, gg, gr = jnp.split(proj, splits, axis=-1)

    def heads(t, nh):
        return t.reshape(B, L, nh, -1).astype(f32)

    z = heads(hf, HG_HEADS)
    lb_h = lb.reshape(HG_HEADS, HG_DK).astype(f32)
    log_f = jnp.logaddexp(jnp.log(lb_h), jnp.log1p(-lb_h) + jax.nn.log_sigmoid(z))
    k_h = (1.0 - lb_h) * jax.nn.sigmoid(-z)
    q_h = jax.nn.silu(heads(hq, HG_HEADS))
    o_h, s_hg_new = gated_recurrence(q_h, k_h, heads(hi, HG_HEADS), log_f, s_hg.astype(f32))
    o_h = rms_norm(o_h, gn_hg) * jax.nn.silu(heads(hg, HG_HEADS))

    q_g = heads(gq, GLA_HEADS) * (GLA_DK ** -0.5)
    k_g = heads(gk, GLA_HEADS)
    v_g = heads(gv, GLA_HEADS)
    log_a = jax.nn.log_sigmoid((gr @ w_gk2 + b_gk2).astype(f32)).reshape(B, L, GLA_HEADS, GLA_DK) / GLA_GATE_NORM
    o_g, s_gl_new = gated_recurrence(q_g, k_g, v_g, log_a, s_gl.astype(f32))
    o_g = rms_norm(o_g, gn_gl) * jax.nn.silu(heads(gg, GLA_HEADS))

    o = jnp.concatenate([o_h.reshape(B, L, -1), o_g.reshape(B, L, -1)], axis=-1).astype(h.dtype)
    return o @ w_o, s_hg_new, s_gl_new


def moe_ffn(h, router_w, router_b, w_up, b_up, w_down, b_down):
    B, L, D = h.shape
    T = B * L
    x2d = h.reshape(T, D)
    logits = (x2d @ router_w + router_b).astype(jnp.float32)
    top_val, top_idx = lax.top_k(logits, TOP_K)
    gate = jax.nn.softmax(top_val, axis=-1)
    M = T * TOP_K
    flat_e = top_idx.reshape(M)
    order = jnp.argsort(flat_e)
    sorted_e = flat_e[order]
    sorted_tok = order // TOP_K
    sorted_gate = gate.reshape(M)[order]
    counts = jnp.bincount(flat_e, length=N_EXPERTS)
    padded = (counts + MOE_BLOCK - 1) // MOE_BLOCK * MOE_BLOCK
    start = jnp.cumsum(counts) - counts
    padded_end = jnp.cumsum(padded)
    padded_start = padded_end - padded
    dest = padded_start[sorted_e] + jnp.arange(M) - start[sorted_e]
    n_blocks = -(-M // MOE_BLOCK) + N_EXPERTS
    rows = n_blocks * MOE_BLOCK
    row_tok = jnp.zeros((rows,), jnp.int32).at[dest].set(sorted_tok)
    block_e = jnp.minimum(jnp.searchsorted(padded_end, jnp.arange(n_blocks) * MOE_BLOCK, side='right'),
                          N_EXPERTS - 1)
    xb = x2d[row_tok].reshape(n_blocks, MOE_BLOCK, D)

    def expert_block(args):
        xblk, e = args
        u = xblk @ w_up[e] + b_up[e]
        glu, lin = jnp.split(u, 2, axis=-1)
        glu = jnp.minimum(glu, SWIGLU_LIMIT)
        lin = jnp.clip(lin, -SWIGLU_LIMIT, SWIGLU_LIMIT)
        act = glu * jax.nn.sigmoid(SWIGLU_ALPHA * glu) * (lin + 1.0)
        return act @ w_down[e] + b_down[e]

    yb = lax.map(expert_block, (xb, block_e)).reshape(rows, D)
    y = jax.ops.segment_sum(yb[dest] * sorted_gate[:, None].astype(yb.dtype), sorted_tok, num_segments=T)
    return y.reshape(B, L, D)


def decoder_layer(x, c, s_hg, s_gl, lb, norm1_g, norm2_g, w_ada, b_ada, w_in, w_gk2, b_gk2,
                  gn_hg, gn_gl, w_o, router_w, router_b, w_up, b_up, w_down, b_down):
    mod = (jax.nn.silu(c) @ w_ada + b_ada)[:, None, :]
    sh1, sc1, g1, sh2, sc2, g2 = jnp.split(mod, 6, axis=-1)
    h = rms_norm(x, norm1_g) * (1.0 + sc1) + sh1
    o, s_hg_new, s_gl_new = token_mixers(h, s_hg, s_gl, lb, w_in, w_gk2, b_gk2, gn_hg, gn_gl, w_o)
    x = x + g1 * o
    h = rms_norm(x, norm2_g) * (1.0 + sc2) + sh2
    x = x + g2 * moe_ffn(h, router_w, router_b, w_up, b_up, w_down, b_down)
    return x, s_hg_new, s_gl_new


def setup_inputs(seed: int = 0) -> dict:
    key = jax.random.key(seed)
    ks = jax.random.split(key, 26)
    D = D_MODEL

    def nrm(k, shape, s):
        return jax.random.normal(k, shape, jnp.float32) * s

    return {
        "x_prompt": nrm(ks[0], (BATCH, SEQ, D), 1.0),
        "x_sample": nrm(ks[1], (DEC_BATCH, DEC_SEQ, D), 1.0),
        "state_hgrn": nrm(ks[2], (DEPTH, DEC_BATCH, HG_HEADS, HG_DK, HG_DV), 0.3),
        "state_gla": nrm(ks[3], (DEPTH, DEC_BATCH, GLA_HEADS, GLA_DK, GLA_DV), 0.3),
        "c_prompt": nrm(ks[4], (BATCH, D), 1.0),
        "c_sample": nrm(ks[5], (DEC_BATCH, D), 1.0),
        "hgrn_lower_bounds": nrm(ks[6], (DEPTH, HG_HEADS * HG_DK), 1.0),
        "norm1_g": 1.0 + nrm(ks[7], (DEPTH, D), 0.02),
        "norm2_g": 1.0 + nrm(ks[8], (DEPTH, D), 0.02),
        "w_ada": nrm(ks[9], (DEPTH, D, 6 * D), 0.3 * D ** -0.5),
        "b_ada": nrm(ks[10], (DEPTH, 6 * D), 0.02),
        "w_in": nrm(ks[11], (DEPTH, D, D_IN_PROJ), D ** -0.5),
        "w_gk2": nrm(ks[12], (DEPTH, GLA_GATE_RANK, GLA_HEADS * GLA_DK), GLA_GATE_RANK ** -0.5),
        "b_gk2": nrm(ks[13], (DEPTH, GLA_HEADS * GLA_DK), 0.1),
        "gn_hgrn": 1.0 + nrm(ks[14], (DEPTH, HG_DV), 0.02),
        "gn_gla": 1.0 + nrm(ks[15], (DEPTH, GLA_DV), 0.02),
        "w_o": nrm(ks[16], (DEPTH, MIX_WIDTH, D), MIX_WIDTH ** -0.5),
        "router_w": nrm(ks[17], (DEPTH, D, N_EXPERTS), D ** -0.5),
        "router_b": nrm(ks[18], (DEPTH, N_EXPERTS), 0.01),
        "w_up": nrm(ks[19], (DEPTH, N_EXPERTS, D, 2 * D_FF), D ** -0.5),
        "b_up": nrm(ks[20], (DEPTH, N_EXPERTS, 2 * D_FF), 0.01),
        "w_down": nrm(ks[21], (DEPTH, N_EXPERTS, D_FF, D), D_FF ** -0.5),
        "b_down": nrm(ks[22], (DEPTH, N_EXPERTS, D), 0.01),
        "final_norm_g": 1.0 + nrm(ks[23], (D,), 0.02),
    }


def reference(x_prompt, x_sample, state_hgrn, state_gla, c_prompt, c_sample, hgrn_lower_bounds,
              norm1_g, norm2_g, w_ada, b_ada, w_in, w_gk2, b_gk2, gn_hgrn, gn_gla, w_o,
              router_w, router_b, w_up, b_up, w_down, b_down, final_norm_g):
    f32 = jnp.float32
    lbs = jnp.cumsum(jax.nn.softmax(hgrn_lower_bounds.astype(f32), axis=0), axis=0)
    lbs = lbs - lbs[0:1]
    B = x_prompt.shape[0]
    yp, ys = x_prompt, x_sample
    hg_p, gl_p, hg_s, gl_s = [], [], [], []
    for l in range(DEPTH):
        layer_w = (lbs[l], norm1_g[l], norm2_g[l], w_ada[l], b_ada[l], w_in[l], w_gk2[l], b_gk2[l],
                   gn_hgrn[l], gn_gla[l], w_o[l], router_w[l], router_b[l], w_up[l], b_up[l],
                   w_down[l], b_down[l])
        zero_hg = jnp.zeros((B, HG_HEADS, HG_DK, HG_DV), f32)
        zero_gl = jnp.zeros((B, GLA_HEADS, GLA_DK, GLA_DV), f32)
        yp, shp, sgp = decoder_layer(yp, c_prompt, zero_hg, zero_gl, *layer_w)
        ys, shs, sgs = decoder_layer(ys, c_sample, state_hgrn[l], state_gla[l], *layer_w)
        hg_p.append(shp.astype(x_prompt.dtype))
        gl_p.append(sgp.astype(x_prompt.dtype))
        hg_s.append(shs.astype(state_hgrn.dtype))
        gl_s.append(sgs.astype(state_gla.dtype))
    y_prompt = rms_norm(yp, final_norm_g)
    y_sample = rms_norm(ys, final_norm_g)
    new_hgrn_prompt = jnp.stack(hg_p)
    new_gla_prompt = jnp.stack(gl_p)
    new_hgrn_sample = jnp.stack(hg_s)
    new_gla_sample = jnp.stack(gl_s)
    return (y_prompt, y_sample, new_hgrn_prompt, new_gla_prompt, new_hgrn_sample, new_gla_sample)
```

```python
import functools

import numpy as np
import jax
import jax.numpy as jnp
from jax import lax
from jax.experimental import pallas as pl
from jax.experimental.pallas import tpu as pltpu

F32 = jnp.float32
BF16 = jnp.bfloat16
I32 = jnp.int32

D_MODEL = 1024
HG_HEADS, HG_DK, HG_DV = 4, 128, 128
GLA_HEADS, GLA_DK, GLA_DV = 4, 64, 128
GLA_GATE_RANK = 16
GLA_GATE_NORM = 16.0
CHUNK = 64
N_EXPERTS = 32
TOP_K = 4
D_FF = 1024
SWIGLU_LIMIT = 7.0
SWIGLU_ALPHA = 1.702
EPS = 1e-6

HQ0, HF0, HI0, HGATE0 = 0, 512, 1024, 1536
GQ0, GK0, GV0, GGATE0 = 2048, 2304, 2560, 3072
W_MAIN = 3584
N_DECAY = HG_HEADS * HG_DK + GLA_HEADS * GLA_DK

LEVELS = (32, 16, 8, 4, 2, 1)
N_MATS = 2 + len(LEVELS)

EXPERT_BLOCK = 256
VMEM_LIMIT = 56 * 1024 * 1024


def _sigmoid(x):
    return 1.0 / (1.0 + jnp.exp(-x))


def _silu(x):
    return x * _sigmoid(x)


def _log_sigmoid(x):
    return jnp.minimum(x, 0.0) - jnp.log1p(jnp.exp(-jnp.abs(x)))


def _rms(x, g):
    return x * lax.rsqrt(jnp.mean(x * x, axis=-1, keepdims=True) + EPS) * g


def _dot(a, b):
    return jnp.dot(a, b, preferred_element_type=F32)


def _dot_nt(a, b):
    return lax.dot_general(a, b, (((1,), (1,)), ((), ())), preferred_element_type=F32)


def _dot_tn(a, b):
    return lax.dot_general(a, b, (((0,), (0,)), ((), ())), preferred_element_type=F32)


def _split3(x):
    hi = x.astype(BF16)
    r = x - hi.astype(F32)
    mid = r.astype(BF16)
    lo = (r - mid.astype(F32)).astype(BF16)
    return hi, mid, lo


def _dot3(m_bf16, x):
    hi, mid, lo = _split3(x)
    return _dot(m_bf16, hi) + _dot(m_bf16, mid) + _dot(m_bf16, lo)


def _params(sem=None):
    return pltpu.CompilerParams(dimension_semantics=sem, vmem_limit_bytes=VMEM_LIMIT)


def _mod_kernel(c_ref, w_ref, b_ref, o_ref):
    s = _silu(c_ref[...]).astype(BF16)
    o_ref[...] = _dot(s, w_ref[...].astype(BF16)) + b_ref[...]


def _modulation(c_all, w_ada, b_ada):
    depth, d, n = w_ada.shape
    rows = c_all.shape[0]
    tn = 1536
    return pl.pallas_call(
        _mod_kernel,
        grid=(depth, n // tn),
        in_specs=[
            pl.BlockSpec((rows, d), lambda l, j: (0, 0)),
            pl.BlockSpec((None, d, tn), lambda l, j: (l, 0, j)),
            pl.BlockSpec((None, 1, tn), lambda l, j: (l, 0, j)),
        ],
        out_specs=pl.BlockSpec((None, rows, tn), lambda l, j: (l, 0, j)),
        out_shape=jax.ShapeDtypeStruct((depth, rows, n), F32),
        compiler_params=_params(("arbitrary", "arbitrary")),
        name="ada_mod",
    )(c_all, w_ada, b_ada.reshape(depth, 1, n))


def _proj_kernel(x_ref, sh_ref, sc_ref, g_ref, w_ref, wr_ref, wg_ref, bg_ref, pm_ref, gl_ref):
    h = _rms(x_ref[...], g_ref[...]) * (1.0 + sc_ref[...]) + sh_ref[...]
    hb = h.astype(BF16)
    pm_ref[...] = _dot(hb, w_ref[...])
    gr = _dot(hb, wr_ref[...])
    gl_ref[...] = _dot(gr.astype(BF16), wg_ref[...]) + bg_ref[...]


def _in_proj(x2d, mod, per_row, rows_per_seq, norm_g, w_main, w_rank, w_gk2, b_gk2, tm):
    t, d = x2d.shape
    if per_row:
        mod_spec = lambda c: pl.BlockSpec((tm, d), lambda i: (i, c))
    else:
        per = rows_per_seq // tm
        mod_spec = lambda c: pl.BlockSpec((None, 1, d), lambda i: (i // per, 0, c))
    ngl = w_gk2.shape[1]
    return pl.pallas_call(
        _proj_kernel,
        grid=(t // tm,),
        in_specs=[
            pl.BlockSpec((tm, d), lambda i: (i, 0)),
            mod_spec(0), mod_spec(1),
            pl.BlockSpec((1, d), lambda i: (0, 0)),
            pl.BlockSpec((d, W_MAIN), lambda i: (0, 0)),
            pl.BlockSpec((d, GLA_GATE_RANK), lambda i: (0, 0)),
            pl.BlockSpec((GLA_GATE_RANK, ngl), lambda i: (0, 0)),
            pl.BlockSpec((1, ngl), lambda i: (0, 0)),
        ],
        out_specs=[
            pl.BlockSpec((tm, W_MAIN), lambda i: (i, 0)),
            pl.BlockSpec((tm, ngl), lambda i: (i, 0)),
        ],
        out_shape=[
            jax.ShapeDtypeStruct((t, W_MAIN), F32),
            jax.ShapeDtypeStruct((t, ngl), F32),
        ],
        compiler_params=_params(("arbitrary",)),
        name="in_proj",
    )(x2d, mod, mod, norm_g, w_main, w_rank, w_gk2, b_gk2)


def _hgrn_log_f(z, lb):
    a1 = jnp.log(lb)
    a2 = jnp.log1p(-lb) + _log_sigmoid(z)
    mx = jnp.maximum(a1, a2)
    mn = jnp.minimum(a1, a2)
    return mx + jnp.log1p(jnp.exp(mn - mx))


def _level_matrices():
    t = np.arange(CHUNK)[:, None]
    u = np.arange(CHUNK)[None, :]
    tri = (u <= t).astype(np.float32)
    mats = [tri, (u > t).astype(np.float32)]
    masks = []
    for c in LEVELS:
        ref = (t // (2 * c)) * (2 * c) + c - 1
        mats.append(tri - (u <= ref).astype(np.float32))
        s = u
        masks.append(((t // (2 * c) == s // (2 * c)) & (t % (2 * c) >= c) & (s % (2 * c) < c)).astype(np.float32))
    return np.concatenate(mats, 0), np.stack(masks, 0)


def _recur_head(q, k, v, dsc_ref, lane0, dk, masks_ref, st_ref, head):
    sl = slice(lane0, lane0 + dk)
    rows = lax.broadcasted_iota(I32, (CHUNK, dk), 0)
    b = dsc_ref[0:CHUNK, sl]
    tail = dsc_ref[CHUNK:2 * CHUNK, sl]
    att = jnp.zeros((CHUNK, CHUNK), F32)
    for li, c in enumerate(LEVELS):
        dl = dsc_ref[(2 + li) * CHUNK:(3 + li) * CHUNK, sl]
        w = (jnp.where((rows & c) != 0, q, k) * jnp.exp(-jnp.abs(dl))).astype(BF16)
        att = att + masks_ref[li] * _dot_nt(w, w)
    vb = v.astype(BF16)
    o = _dot(att.astype(BF16), vb) + jnp.sum(q * k, axis=1, keepdims=True) * v
    st = st_ref[head]
    o = o + _dot_nt((q * jnp.exp(b)).astype(BF16), st.astype(BF16))
    khat = (k * jnp.exp(tail)).astype(BF16)
    b_last = dsc_ref[CHUNK - 1:CHUNK, sl]
    st_ref[head] = st * jnp.exp(b_last) + _dot_tn(vb, khat)
    return o


def _recur_kernel(pm_ref, gl_ref, lb_ref, gnh_ref, gng_ref, mats_ref, masks_ref,
                  o_ref, sh_ref, sg_ref, sth_ref, stg_ref, dsc_ref, *, n_chunks):
    s = pl.program_id(1)

    @pl.when(s == 0)
    def _():
        sth_ref[...] = jnp.zeros_like(sth_ref)
        stg_ref[...] = jnp.zeros_like(stg_ref)

    lb = lb_ref[...]
    one_m_lb = 1.0 - lb

    def chunk(ci, carry):
        r0 = pl.multiple_of(ci * CHUNK, CHUNK)
        rs = pl.ds(r0, CHUNK)
        z = pm_ref[rs, HF0:HF0 + 512]
        lf = _hgrn_log_f(z, lb)
        la = _log_sigmoid(gl_ref[rs, :]) / GLA_GATE_NORM
        g = jnp.concatenate([lf, la], axis=1)
        dsc_ref[...] = _dot3(mats_ref[...], g)
        kh_all = one_m_lb * _sigmoid(-z)
        for h in range(HG_HEADS):
            l0 = h * HG_DK
            q = _silu(pm_ref[rs, HQ0 + l0:HQ0 + l0 + HG_DK])
            k = kh_all[:, l0:l0 + HG_DK]
            v = pm_ref[rs, HI0 + h * HG_DV:HI0 + (h + 1) * HG_DV]
            o = _recur_head(q, k, v, dsc_ref, l0, HG_DK, masks_ref, sth_ref, h)
            gate = pm_ref[rs, HGATE0 + h * HG_DV:HGATE0 + (h + 1) * HG_DV]
            o_ref[rs, h * HG_DV:(h + 1) * HG_DV] = (_rms(o, gnh_ref[...]) * _silu(gate)).astype(o_ref.dtype)
        for h in range(GLA_HEADS):
            q = pm_ref[rs, GQ0 + h * GLA_DK:GQ0 + (h + 1) * GLA_DK] * (GLA_DK ** -0.5)
            k = pm_ref[rs, GK0 + h * GLA_DK:GK0 + (h + 1) * GLA_DK]
            v = pm_ref[rs, GV0 + h * GLA_DV:GV0 + (h + 1) * GLA_DV]
            o = _recur_head(q, k, v, dsc_ref, HG_HEADS * HG_DK + h * GLA_DK, GLA_DK, masks_ref, stg_ref, h)
            gate = pm_ref[rs, GGATE0 + h * GLA_DV:GGATE0 + (h + 1) * GLA_DV]
            c0 = HG_HEADS * HG_DV + h * GLA_DV
            o_ref[rs, c0:c0 + GLA_DV] = (_rms(o, gng_ref[...]) * _silu(gate)).astype(o_ref.dtype)
        return carry

    lax.fori_loop(0, n_chunks, chunk, 0)

    @pl.when(s == pl.num_programs(1) - 1)
    def _():
        sh_ref[...] = sth_ref[...]
        sg_ref[...] = stg_ref[...]


def _prompt_recurrence(pm, gl, lb, gn_h, gn_g, batch, seq, ts):
    mats, masks = _level_matrices()
    steps = seq // ts
    kern = functools.partial(_recur_kernel, n_chunks=ts // CHUNK)
    return pl.pallas_call(
        kern,
        grid=(batch, steps),
        in_specs=[
            pl.BlockSpec((ts, W_MAIN), lambda b, s: (b * steps + s, 0)),
            pl.BlockSpec((ts, GLA_HEADS * GLA_DK), lambda b, s: (b * steps + s, 0)),
            pl.BlockSpec((1, HG_HEADS * HG_DK), lambda b, s: (0, 0)),
            pl.BlockSpec((1, HG_DV), lambda b, s: (0, 0)),
            pl.BlockSpec((1, GLA_DV), lambda b, s: (0, 0)),
            pl.BlockSpec((N_MATS * CHUNK, CHUNK), lambda b, s: (0, 0)),
            pl.BlockSpec((len(LEVELS), CHUNK, CHUNK), lambda b, s: (0, 0, 0)),
        ],
        out_specs=[
            pl.BlockSpec((ts, D_MODEL), lambda b, s: (b * steps + s, 0)),
            pl.BlockSpec((None, HG_HEADS, HG_DV, HG_DK), lambda b, s: (b, 0, 0, 0)),
            pl.BlockSpec((None, GLA_HEADS, GLA_DV, GLA_DK), lambda b, s: (b, 0, 0, 0)),
        ],
        out_shape=[
            jax.ShapeDtypeStruct((batch * seq, D_MODEL), BF16),
            jax.ShapeDtypeStruct((batch, HG_HEADS, HG_DV, HG_DK), F32),
            jax.ShapeDtypeStruct((batch, GLA_HEADS, GLA_DV, GLA_DK), F32),
        ],
        scratch_shapes=[
            pltpu.VMEM((HG_HEADS, HG_DV, HG_DK), F32),
            pltpu.VMEM((GLA_HEADS, GLA_DV, GLA_DK), F32),
            pltpu.VMEM((N_MATS * CHUNK, N_DECAY), F32),
        ],
        compiler_params=_params(("arbitrary", "arbitrary")),
        name="prompt_recurrence",
    )(pm, gl, lb, gn_h, gn_g, jnp.asarray(mats, BF16), jnp.asarray(masks, F32))


def _column(row, eye):
    return jnp.sum(jnp.where(eye, row, 0.0), axis=1, keepdims=True)


def _sample_kernel(pm_ref, gl_ref, lb_ref, gnh_ref, gng_ref, sh_ref, sg_ref,
                   o_ref, nsh_ref, nsg_ref, oscr_ref, *, sb):
    lb = lb_ref[...]
    z = pm_ref[:, HF0:HF0 + 512]
    a_h = jnp.exp(_hgrn_log_f(z, lb))
    k_h = (1.0 - lb) * _sigmoid(-z)
    q_h = _silu(pm_ref[:, HQ0:HQ0 + 512])
    a_g = jnp.exp(_log_sigmoid(gl_ref[...]) / GLA_GATE_NORM)
    q_g = pm_ref[:, GQ0:GQ0 + 256] * (GLA_DK ** -0.5)
    k_g = pm_ref[:, GK0:GK0 + 256]

    def eye(n):
        return lax.broadcasted_iota(I32, (n, n), 0) == lax.broadcasted_iota(I32, (n, n), 1)

    eye_h, eye_g = eye(HG_DK), eye(GLA_DK)
    for j in range(sb):
        for h in range(HG_HEADS):
            sl = slice(h * HG_DK, (h + 1) * HG_DK)
            v = pm_ref[j:j + 1, HI0 + h * HG_DV:HI0 + (h + 1) * HG_DV]
            s_new = (_column(a_h[j:j + 1, sl], eye_h) * sh_ref[j, h]
                     + _column(k_h[j:j + 1, sl], eye_h) * v)
            nsh_ref[j, h] = s_new
            oscr_ref[j:j + 1, h * HG_DV:(h + 1) * HG_DV] = jnp.sum(
                _column(q_h[j:j + 1, sl], eye_h) * s_new, axis=0, keepdims=True)
        for h in range(GLA_HEADS):
            sl = slice(h * GLA_DK, (h + 1) * GLA_DK)
            v = pm_ref[j:j + 1, GV0 + h * GLA_DV:GV0 + (h + 1) * GLA_DV]
            s_new = (_column(a_g[j:j + 1, sl], eye_g) * sg_ref[j, h]
                     + _column(k_g[j:j + 1, sl], eye_g) * v)
            nsg_ref[j, h] = s_new
            c0 = HG_HEADS * HG_DV + h * GLA_DV
            oscr_ref[j:j + 1, c0:c0 + GLA_DV] = jnp.sum(
                _column(q_g[j:j + 1, sl], eye_g) * s_new, axis=0, keepdims=True)
    for h in range(HG_HEADS + GLA_HEADS):
        sl = slice(h * 128, (h + 1) * 128)
        gn = gnh_ref[...] if h < HG_HEADS else gng_ref[...]
        g0 = HGATE0 + h * 128 if h < HG_HEADS else GGATE0 + (h - HG_HEADS) * 128
        gate = pm_ref[:, g0:g0 + 128]
        o_ref[:, sl] = (_rms(oscr_ref[:, sl], gn) * _silu(gate)).astype(o_ref.dtype)


def _sample_recurrence(pm, gl, lb, gn_h, gn_g, s_h, s_g, sb):
    n = pm.shape[0]
    kern = functools.partial(_sample_kernel, sb=sb)
    return pl.pallas_call(
        kern,
        grid=(n // sb,),
        in_specs=[
            pl.BlockSpec((sb, W_MAIN), lambda i: (i, 0)),
            pl.BlockSpec((sb, GLA_HEADS * GLA_DK), lambda i: (i, 0)),
            pl.BlockSpec((1, HG_HEADS * HG_DK), lambda i: (0, 0)),
            pl.BlockSpec((1, HG_DV), lambda i: (0, 0)),
            pl.BlockSpec((1, GLA_DV), lambda i: (0, 0)),
            pl.BlockSpec((sb, HG_HEADS, HG_DK, HG_DV), lambda i: (i, 0, 0, 0)),
            pl.BlockSpec((sb, GLA_HEADS, GLA_DK, GLA_DV), lambda i: (i, 0, 0, 0)),
        ],
        out_specs=[
            pl.BlockSpec((sb, D_MODEL), lambda i: (i, 0)),
            pl.BlockSpec((sb, HG_HEADS, HG_DK, HG_DV), lambda i: (i, 0, 0, 0)),
            pl.BlockSpec((sb, GLA_HEADS, GLA_DK, GLA_DV), lambda i: (i, 0, 0, 0)),
        ],
        out_shape=[
            jax.ShapeDtypeStruct((n, D_MODEL), BF16),
            jax.ShapeDtypeStruct(s_h.shape, F32),
            jax.ShapeDtypeStruct(s_g.shape, F32),
        ],
        scratch_shapes=[pltpu.VMEM((sb, D_MODEL), F32)],
        compiler_params=_params(("arbitrary",)),
        name="sample_recurrence",
    )(pm, gl, lb, gn_h, gn_g, s_h, s_g)


def _post_kernel(x_ref, o_ref, g1_ref, sh_ref, sc_ref, ng_ref, wo_ref, rw_ref, rb_ref,
                 x1_ref, h2_ref, ti_ref, tg_ref):
    x1 = x_ref[...] + g1_ref[...] * _dot(o_ref[...], wo_ref[...])
    x1_ref[...] = x1
    h2 = _rms(x1, ng_ref[...]) * (1.0 + sc_ref[...]) + sh_ref[...]
    h2_ref[...] = h2
    logits = _dot3_rhs(h2, rw_ref[...]) + rb_ref[...]
    lanes = lax.broadcasted_iota(I32, logits.shape, 1).astype(F32)
    vals = logits
    tops, idxs = [], []
    for _ in range(TOP_K):
        m = jnp.max(vals, axis=1, keepdims=True)
        idx = jnp.min(jnp.where(vals == m, lanes, float(N_EXPERTS)), axis=1, keepdims=True)
        tops.append(m)
        idxs.append(idx)
        vals = jnp.where(lanes == idx, -jnp.inf, vals)
    es = [jnp.exp(t - tops[0]) for t in tops]
    den = es[0] + es[1] + es[2] + es[3]
    ti_ref[...] = jnp.concatenate(idxs, axis=1).astype(I32)
    tg_ref[...] = jnp.concatenate([e / den for e in es], axis=1)


def _dot3_rhs(x, w):
    xh, xm, xl = _split3(x)
    wh, wm, wl = _split3(w)
    return (_dot(xh, wh) + (_dot(xh, wm) + _dot(xm, wh))
            + (_dot(xh, wl) + _dot(xm, wm) + _dot(xl, wh)))


def _post(x2d, o2d, mod, per_row, rows_per_seq, norm_g, w_o, router_w, router_b, tm):
    t, d = x2d.shape
    if per_row:
        mod_spec = lambda c: pl.BlockSpec((tm, d), lambda i: (i, c))
    else:
        per = rows_per_seq // tm
        mod_spec = lambda c: pl.BlockSpec((None, 1, d), lambda i: (i // per, 0, c))
    return pl.pallas_call(
        _post_kernel,
        grid=(t // tm,),
        in_specs=[
            pl.BlockSpec((tm, d), lambda i: (i, 0)),
            pl.BlockSpec((tm, d), lambda i: (i, 0)),
            mod_spec(2), mod_spec(3), mod_spec(4),
            pl.BlockSpec((1, d), lambda i: (0, 0)),
            pl.BlockSpec((d, d), lambda i: (0, 0)),
            pl.BlockSpec((d, N_EXPERTS), lambda i: (0, 0)),
            pl.BlockSpec((1, N_EXPERTS), lambda i: (0, 0)),
        ],
        out_specs=[
            pl.BlockSpec((tm, d), lambda i: (i, 0)),
            pl.BlockSpec((tm, d), lambda i: (i, 0)),
            pl.BlockSpec((tm, TOP_K), lambda i: (i, 0)),
            pl.BlockSpec((tm, TOP_K), lambda i: (i, 0)),
        ],
        out_shape=[
            jax.ShapeDtypeStruct((t, d), F32),
            jax.ShapeDtypeStruct((t, d), F32),
            jax.ShapeDtypeStruct((t, TOP_K), I32),
            jax.ShapeDtypeStruct((t, TOP_K), F32),
        ],
        compiler_params=_params(("arbitrary",)),
        name="post_mixer",
    )(x2d, o2d, mod, mod, mod, norm_g, w_o, router_w, router_b)


def _route_kernel(ti_ref, dest_ref, be_ref, nu_ref, *, n_tok_blocks, tb, n_row_blocks):
    lanes = lax.broadcasted_iota(I32, (tb, N_EXPERTS), 1)

    def onehots(i):
        r0 = pl.multiple_of(i * tb, tb)
        ti = ti_ref[pl.ds(r0, tb), :]
        return [(ti[:, k:k + 1] == lanes).astype(F32) for k in range(TOP_K)]

    def count(i, acc):
        oh = onehots(i)
        return acc + jnp.sum(oh[0] + oh[1] + oh[2] + oh[3], axis=0, keepdims=True)

    counts = lax.fori_loop(0, n_tok_blocks, count, jnp.zeros((1, N_EXPERTS), F32))
    blocks = jnp.floor((counts + (EXPERT_BLOCK - 1)) * (1.0 / EXPERT_BLOCK))
    padded = blocks * EXPERT_BLOCK
    e_r = lax.broadcasted_iota(I32, (N_EXPERTS, N_EXPERTS), 0)
    e_c = lax.broadcasted_iota(I32, (N_EXPERTS, N_EXPERTS), 1)
    upper = (e_r <= e_c).astype(BF16)
    padded_end = _dot3_lhs(padded, upper)
    padded_start = padded_end - padded

    t_r = lax.broadcasted_iota(I32, (tb, tb), 0)
    t_c = lax.broadcasted_iota(I32, (tb, tb), 1)
    strict = (t_c < t_r).astype(BF16)

    def place(i, carry):
        oh = onehots(i)
        c = oh[0] + oh[1] + oh[2] + oh[3]
        rank = _dot(strict, c.astype(BF16)) + carry
        cols = [jnp.sum(oh[k] * rank, axis=1, keepdims=True) for k in range(TOP_K)]
        r0 = pl.multiple_of(i * tb, tb)
        dest_ref[pl.ds(r0, tb), :] = jnp.concatenate(cols, axis=1).astype(I32)
        return carry + jnp.sum(c, axis=0, keepdims=True)

    lax.fori_loop(0, n_tok_blocks, place, padded_start)

    starts = (lax.broadcasted_iota(I32, (n_row_blocks, N_EXPERTS), 0) * EXPERT_BLOCK).astype(F32)
    be = jnp.sum((starts >= padded_end).astype(F32), axis=1, keepdims=True)
    be_ref[...] = jnp.minimum(be, N_EXPERTS - 1.0).astype(I32)
    nu_ref[...] = (padded_end[:, N_EXPERTS - 1:N_EXPERTS] * (1.0 / EXPERT_BLOCK)).astype(I32)


def _dot3_lhs(x, m_bf16):
    hi, mid, lo = _split3(x)
    return _dot(hi, m_bf16) + _dot(mid, m_bf16) + _dot(lo, m_bf16)


def _route(top_idx, n_row_blocks):
    t = top_idx.shape[0]
    tb = 128
    nbp = -(-n_row_blocks // 8) * 8
    kern = functools.partial(_route_kernel, n_tok_blocks=t // tb, tb=tb, n_row_blocks=nbp)
    dest, be, nu = pl.pallas_call(
        kern,
        out_shape=[
            jax.ShapeDtypeStruct((t, TOP_K), I32),
            jax.ShapeDtypeStruct((nbp, 1), I32),
            jax.ShapeDtypeStruct((1, 1), I32),
        ],
        compiler_params=_params(),
        name="route",
    )(top_idx)
    return dest, be[:n_row_blocks, 0], nu[0]


def _dispatch_kernel(dest_ref, h_ref, xin_ref, xb_ref, sem, *, tb):
    del xin_ref
    base = pl.program_id(0) * (tb * TOP_K)

    def row_copy(t, d):
        return pltpu.make_async_copy(h_ref.at[pl.ds(t, 1)], xb_ref.at[pl.ds(d, 1)], sem)

    def start(t, carry):
        for k in range(TOP_K):
            row_copy(t, dest_ref[base + t * TOP_K + k]).start()
        return carry

    def wait(t, carry):
        for k in range(TOP_K):
            row_copy(0, 0).wait()
        return carry

    lax.fori_loop(0, tb, start, 0)
    lax.fori_loop(0, tb, wait, 0)


def _dispatch(dest_flat, h2, xb, tb):
    t, d = h2.shape
    kern = functools.partial(_dispatch_kernel, tb=tb)
    return pl.pallas_call(
        kern,
        grid_spec=pltpu.PrefetchScalarGridSpec(
            num_scalar_prefetch=1,
            grid=(t // tb,),
            in_specs=[
                pl.BlockSpec((tb, d), lambda i, dest: (i, 0)),
                pl.BlockSpec(memory_space=pl.ANY),
            ],
            out_specs=pl.BlockSpec(memory_space=pl.ANY),
            scratch_shapes=[pltpu.SemaphoreType.DMA(())],
        ),
        out_shape=jax.ShapeDtypeStruct(xb.shape, xb.dtype),
        input_output_aliases={2: 0},
        compiler_params=_params(("arbitrary",)),
        name="dispatch",
    )(dest_flat, h2, xb)


def _expert_kernel(be_ref, nu_ref, x_ref, wu_ref, bu_ref, wd_ref, bd_ref, y_ref, wu_bf, wd_bf):
    i = pl.program_id(0)
    e = be_ref[i]
    prev = be_ref[jnp.maximum(i - 1, 0)]

    @pl.when((i == 0) | (e != prev))
    def _():
        wu_bf[...] = wu_ref[...].astype(BF16)
        wd_bf[...] = wd_ref[...].astype(BF16)

    @pl.when(i < nu_ref[0])
    def _():
        u = _dot(x_ref[...].astype(BF16), wu_bf[...]) + bu_ref[...]
        glu = jnp.minimum(u[:, :D_FF], SWIGLU_LIMIT)
        lin = jnp.clip(u[:, D_FF:], -SWIGLU_LIMIT, SWIGLU_LIMIT)
        act = glu * _sigmoid(SWIGLU_ALPHA * glu) * (lin + 1.0)
        y_ref[...] = _dot(act.astype(BF16), wd_bf[...]) + bd_ref[...]

    @pl.when(i >= nu_ref[0])
    def _():
        y_ref[...] = jnp.zeros_like(y_ref)


def _experts(block_e, n_used, xb, w_up, b_up, w_down, b_down):
    rows, d = xb.shape
    nb = rows // EXPERT_BLOCK
    ne = w_up.shape[0]
    return pl.pallas_call(
        _expert_kernel,
        grid_spec=pltpu.PrefetchScalarGridSpec(
            num_scalar_prefetch=2,
            grid=(nb,),
            in_specs=[
                pl.BlockSpec((EXPERT_BLOCK, d), lambda i, be, nu: (i, 0)),
                pl.BlockSpec((None, d, 2 * D_FF), lambda i, be, nu: (be[i], 0, 0)),
                pl.BlockSpec((None, 1, 2 * D_FF), lambda i, be, nu: (be[i], 0, 0)),
                pl.BlockSpec((None, D_FF, d), lambda i, be, nu: (be[i], 0, 0)),
                pl.BlockSpec((None, 1, d), lambda i, be, nu: (be[i], 0, 0)),
            ],
            out_specs=pl.BlockSpec((EXPERT_BLOCK, d), lambda i, be, nu: (i, 0)),
            scratch_shapes=[
                pltpu.VMEM((d, 2 * D_FF), BF16),
                pltpu.VMEM((D_FF, d), BF16),
            ],
        ),
        out_shape=jax.ShapeDtypeStruct((rows, d), F32),
        compiler_params=_params(("arbitrary",)),
        name="experts",
    )(block_e, n_used, xb, w_up, b_up.reshape(ne, 1, 2 * D_FF), w_down, b_down.reshape(ne, 1, d))


def _combine_kernel(dest_ref, yb_ref, tg_ref, x1_ref, g2_ref, fg_ref, out_ref, buf, sem, *, tb, final):
    base = pl.program_id(0) * (tb * TOP_K)

    def row_copy(t, k, d):
        return pltpu.make_async_copy(yb_ref.at[pl.ds(d, 1)], buf.at[k, pl.ds(t, 1)], sem)

    def start(t, carry):
        for k in range(TOP_K):
            row_copy(t, k, dest_ref[base + t * TOP_K + k]).start()
        return carry

    def wait(t, carry):
        for k in range(TOP_K):
            row_copy(0, k, 0).wait()
        return carry

    lax.fori_loop(0, tb, start, 0)
    lax.fori_loop(0, tb, wait, 0)
    tg = tg_ref[...]
    y = tg[:, 0:1] * buf[0]
    for k in range(1, TOP_K):
        y = y + tg[:, k:k + 1] * buf[k]
    x2 = x1_ref[...] + g2_ref[...] * y
    out_ref[...] = _rms(x2, fg_ref[...]) if final else x2


def _combine(dest_flat, yb, gate, x1, mod, per_row, rows_per_seq, final_g, final, tb):
    t, d = x1.shape
    if per_row:
        mod_spec = pl.BlockSpec((tb, d), lambda i, dest: (i, 5))
    else:
        per = rows_per_seq // tb
        mod_spec = pl.BlockSpec((None, 1, d), lambda i, dest: (i // per, 0, 5))
    kern = functools.partial(_combine_kernel, tb=tb, final=final)
    return pl.pallas_call(
        kern,
        grid_spec=pltpu.PrefetchScalarGridSpec(
            num_scalar_prefetch=1,
            grid=(t // tb,),
            in_specs=[
                pl.BlockSpec(memory_space=pl.ANY),
                pl.BlockSpec((tb, TOP_K), lambda i, dest: (i, 0)),
                pl.BlockSpec((tb, d), lambda i, dest: (i, 0)),
                mod_spec,
                pl.BlockSpec((1, d), lambda i, dest: (0, 0)),
            ],
            out_specs=pl.BlockSpec((tb, d), lambda i, dest: (i, 0)),
            scratch_shapes=[
                pltpu.VMEM((TOP_K, tb, d), F32),
                pltpu.SemaphoreType.DMA(()),
            ],
        ),
        out_shape=jax.ShapeDtypeStruct((t, d), F32),
        compiler_params=_params(("arbitrary",)),
        name="combine",
    )(dest_flat, yb, gate, x1, mod, final_g)


def kernel(x_prompt, x_sample, state_hgrn, state_gla, c_prompt, c_sample, hgrn_lower_bounds, norm1_g, norm2_g, w_ada, b_ada, w_in, w_gk2, b_gk2, gn_hgrn, gn_gla, w_o, router_w, router_b, w_up, b_up, w_down, b_down, final_norm_g):
    depth = w_in.shape[0]
    batch, seq, d = x_prompt.shape
    n_s = x_sample.shape[0]
    assert x_sample.shape[1] == 1 and d == D_MODEL
    t_p = batch * seq
    tm_p = min(512, seq)
    tm_s = n_s
    tb = 128
    assert seq % tm_p == 0 and seq % tb == 0 and n_s % tb == 0 and n_s % 8 == 0

    lbs = jnp.cumsum(jax.nn.softmax(hgrn_lower_bounds.astype(F32), axis=0), axis=0)
    lbs = lbs - lbs[0:1]

    mod = _modulation(jnp.concatenate([c_prompt, c_sample], axis=0), w_ada, b_ada)
    w_main = w_in[:, :, :W_MAIN].astype(BF16)
    w_rank = w_in[:, :, W_MAIN:].astype(BF16)
    w_gk2b = w_gk2.astype(BF16)
    w_ob = w_o.astype(BF16)

    m_slots = (t_p + n_s) * TOP_K
    n_row_blocks = -(-m_slots // EXPERT_BLOCK) + N_EXPERTS
    rows = n_row_blocks * EXPERT_BLOCK

    xp = x_prompt.reshape(t_p, d)
    xs = x_sample.reshape(n_s, d)
    hg_p, gl_p, hg_s, gl_s = [], [], [], []
    for l in range(depth):
        mod_p = mod[l, :batch].reshape(batch, 1, 6 * d)
        mod_s = mod[l, batch:]
        lb = lbs[l].reshape(1, -1)
        n1 = norm1_g[l].reshape(1, d)
        n2 = norm2_g[l].reshape(1, d)
        gnh = gn_hgrn[l].reshape(1, -1)
        gng = gn_gla[l].reshape(1, -1)
        bg = b_gk2[l].reshape(1, -1)
        rb = router_b[l].reshape(1, -1)
        final = l == depth - 1
        fg = final_norm_g.reshape(1, d)

        pm_p, glog_p = _in_proj(xp, mod_p, False, seq, n1, w_main[l], w_rank[l], w_gk2b[l], bg, tm_p)
        pm_s, glog_s = _in_proj(xs, mod_s, True, 1, n1, w_main[l], w_rank[l], w_gk2b[l], bg, tm_s)

        o_p, sh_p, sg_p = _prompt_recurrence(pm_p, glog_p, lb, gnh, gng, batch, seq, tm_p)
        o_s, sh_s, sg_s = _sample_recurrence(pm_s, glog_s, lb, gnh, gng, state_hgrn[l], state_gla[l], 8)
        hg_p.append(jnp.swapaxes(sh_p, -1, -2))
        gl_p.append(jnp.swapaxes(sg_p, -1, -2))
        hg_s.append(sh_s)
        gl_s.append(sg_s)

        x1_p, h2_p, ti_p, tg_p = _post(xp, o_p, mod_p, False, seq, n2, w_ob[l], router_w[l], rb, tm_p)
        x1_s, h2_s, ti_s, tg_s = _post(xs, o_s, mod_s, True, 1, n2, w_ob[l], router_w[l], rb, tm_s)

        dest, block_e, n_used = _route(jnp.concatenate([ti_p, ti_s], axis=0), n_row_blocks)
        dest_p = dest[:t_p].reshape(-1)
        dest_s = dest[t_p:].reshape(-1)
        xb = jnp.zeros((rows, d), F32)
        xb = _dispatch(dest_p, h2_p, xb, tb)
        xb = _dispatch(dest_s, h2_s, xb, tb)
        yb = _experts(block_e, n_used, xb, w_up[l], b_up[l], w_down[l], b_down[l])
        xp = _combine(dest_p, yb, tg_p, x1_p, mod_p, False, seq, fg, final, tb)
        xs = _combine(dest_s, yb, tg_s, x1_s, mod_s, True, 1, fg, final, tb)

    return (xp.reshape(batch, seq, d), xs.reshape(n_s, 1, d),
            jnp.stack(hg_p), jnp.stack(gl_p), jnp.stack(hg_s), jnp.stack(gl_s))
```

```python
import functools

import numpy as np
import jax
import jax.numpy as jnp
from jax import lax
from jax.experimental import pallas as pl
from jax.experimental.pallas import tpu as pltpu

F32 = jnp.float32
BF16 = jnp.bfloat16
I32 = jnp.int32

D_MODEL = 1024
HG_HEADS, HG_DK, HG_DV = 4, 128, 128
GLA_HEADS, GLA_DK, GLA_DV = 4, 64, 128
GLA_GATE_RANK = 16
GLA_GATE_NORM = 16.0
CHUNK = 64
N_EXPERTS = 32
TOP_K = 4
D_FF = 1024
SWIGLU_LIMIT = 7.0
SWIGLU_ALPHA = 1.702
EPS = 1e-6
LANES = 128
SUBLANES = 8

HQ0, HF0, HI0, HGATE0 = 0, 512, 1024, 1536
GQ0, GK0, GV0, GGATE0 = 2048, 2304, 2560, 3072
W_MAIN = 3584
P_QH, P_LF, P_KH, P_GH, P_QG, P_KG, P_GG, P_LA = 0, 512, 1024, 1536, 2048, 2304, 2560, 3072
W_PREP = 3328
N_DECAY = HG_HEADS * HG_DK + GLA_HEADS * GLA_DK
N_UNITS = N_DECAY // LANES

LEVELS = (32, 16, 8, 4, 2, 1)
N_MATS = 2 + len(LEVELS)

EXPERT_BLOCK = 512
VMEM_LIMIT = 56 * 1024 * 1024


def _sigmoid(x):
    return 1.0 / (1.0 + jnp.exp(-x))


def _silu(x):
    return x * _sigmoid(x)


def _log_sigmoid(x):
    return jnp.minimum(x, 0.0) - jnp.log1p(jnp.exp(-jnp.abs(x)))


def _rms(x, g):
    return x * lax.rsqrt(jnp.mean(x * x, axis=-1, keepdims=True) + EPS) * g


def _dot(a, b):
    return jnp.dot(a, b, preferred_element_type=F32)


def _dot_nt(a, b):
    return lax.dot_general(a, b, (((1,), (1,)), ((), ())), preferred_element_type=F32)


def _dot_tn(a, b):
    return lax.dot_general(a, b, (((0,), (0,)), ((), ())), preferred_element_type=F32)


def _split3(x):
    hi = x.astype(BF16)
    r = x - hi.astype(F32)
    mid = r.astype(BF16)
    lo = (r - mid.astype(F32)).astype(BF16)
    return hi, mid, lo


def _params(sem=None):
    return pltpu.CompilerParams(dimension_semantics=sem, vmem_limit_bytes=VMEM_LIMIT)


def _mod_kernel(c_ref, w_ref, b_ref, o_ref):
    s = _silu(c_ref[...]).astype(BF16)
    o_ref[...] = _dot(s, w_ref[...].astype(BF16)) + b_ref[...]


def _modulation(c_all, w_ada, b_ada):
    depth, d, n = w_ada.shape
    rows = c_all.shape[0]
    tn = 1536
    return pl.pallas_call(
        _mod_kernel,
        grid=(depth, n // tn),
        in_specs=[
            pl.BlockSpec((rows, d), lambda l, j: (0, 0)),
            pl.BlockSpec((None, d, tn), lambda l, j: (l, 0, j)),
            pl.BlockSpec((None, 1, tn), lambda l, j: (l, 0, j)),
        ],
        out_specs=pl.BlockSpec((None, rows, tn), lambda l, j: (l, 0, j)),
        out_shape=jax.ShapeDtypeStruct((depth, rows, n), F32),
        compiler_params=_params(("arbitrary", "arbitrary")),
        name="ada_mod",
    )(c_all, w_ada, b_ada.reshape(depth, 1, n))


def _hgrn_log_f(z, lb):
    a1 = jnp.log(lb)
    a2 = jnp.log1p(-lb) + _log_sigmoid(z)
    mx = jnp.maximum(a1, a2)
    mn = jnp.minimum(a1, a2)
    return mx + jnp.log1p(jnp.exp(mn - mx))


def _proj_kernel(x_ref, sh_ref, sc_ref, g_ref, lb_ref, w_ref, wr_ref, wg_ref, bg_ref, pm_ref, v_ref):
    h = _rms(x_ref[...], g_ref[...]) * (1.0 + sc_ref[...]) + sh_ref[...]
    hb = h.astype(BF16)
    lb = lb_ref[...]

    def seg(c0, n):
        return _dot(hb, w_ref[:, c0:c0 + n])

    pm_ref[:, P_QH:P_QH + 512] = _silu(seg(HQ0, 512))
    z = seg(HF0, 512)
    pm_ref[:, P_LF:P_LF + 512] = _hgrn_log_f(z, lb)
    pm_ref[:, P_KH:P_KH + 512] = (1.0 - lb) * _sigmoid(-z)
    v_ref[:, 0:512] = seg(HI0, 512).astype(BF16)
    pm_ref[:, P_GH:P_GH + 512] = _silu(seg(HGATE0, 512))
    pm_ref[:, P_QG:P_QG + 256] = seg(GQ0, 256) * (GLA_DK ** -0.5)
    pm_ref[:, P_KG:P_KG + 256] = seg(GK0, 256)
    v_ref[:, 512:1024] = seg(GV0, 512).astype(BF16)
    pm_ref[:, P_GG:P_GG + 512] = _silu(seg(GGATE0, 512))
    gr = _dot(hb, wr_ref[...])
    glog = _dot(gr.astype(BF16), wg_ref[...]) + bg_ref[...]
    pm_ref[:, P_LA:P_LA + 256] = _log_sigmoid(glog) / GLA_GATE_NORM


def _in_proj(x2d, mod, per_row, rows_per_seq, norm_g, lb, layer, w_main, w_rank, w_gk2, b_gk2, tm):
    t, d = x2d.shape
    if per_row:
        mod_spec = lambda c: pl.BlockSpec((tm, d), lambda i: (i, c))
    else:
        per = rows_per_seq // tm
        mod_spec = lambda c: pl.BlockSpec((None, 1, d), lambda i: (i // per, 0, c))
    ngl = w_gk2.shape[1]
    return pl.pallas_call(
        _proj_kernel,
        grid=(t // tm,),
        in_specs=[
            pl.BlockSpec((tm, d), lambda i: (i, 0)),
            mod_spec(0), mod_spec(1),
            pl.BlockSpec((1, d), lambda i: (0, 0)),
            pl.BlockSpec((1, HG_HEADS * HG_DK), lambda i: (0, 0)),
            pl.BlockSpec((None, d, W_MAIN), lambda i: (layer, 0, 0)),
            pl.BlockSpec((None, d, GLA_GATE_RANK), lambda i: (layer, 0, 0)),
            pl.BlockSpec((GLA_GATE_RANK, ngl), lambda i: (0, 0)),
            pl.BlockSpec((1, ngl), lambda i: (0, 0)),
        ],
        out_specs=[
            pl.BlockSpec((tm, W_PREP), lambda i: (i, 0)),
            pl.BlockSpec((tm, D_MODEL), lambda i: (i, 0)),
        ],
        out_shape=[
            jax.ShapeDtypeStruct((t, W_PREP), F32),
            jax.ShapeDtypeStruct((t, D_MODEL), BF16),
        ],
        compiler_params=_params(("arbitrary",)),
        name="in_proj",
    )(x2d, mod, mod, norm_g, lb, w_main, w_rank, w_gk2, b_gk2)


def _level_matrices():
    t = np.arange(CHUNK)[:, None]
    u = np.arange(CHUNK)[None, :]
    tri = (u <= t).astype(np.float32)
    mats = [tri, (u > t).astype(np.float32)]
    masks = []
    for c in LEVELS:
        ref = (t // (2 * c)) * (2 * c) + c - 1
        mats.append(tri - (u <= ref).astype(np.float32))
        masks.append(((t // (2 * c) == u // (2 * c)) & (t % (2 * c) >= c) & (u % (2 * c) < c)).astype(np.float32))
    m = np.concatenate(mats, 0)
    return np.concatenate([m, m, m], 1), np.stack(masks, 0)


def _unit_heads(u):
    if u < HG_HEADS:
        return [(u * HG_DV, P_GH + u * HG_DV, None)]
    heads = []
    for j in range(LANES // GLA_DK):
        h = (u - HG_HEADS) * (LANES // GLA_DK) + j
        heads.append((HG_HEADS * HG_DV + h * GLA_DV, P_GG + h * GLA_DV, j))
    return heads


def _recur_unit(u, rs, pm_ref, v_ref, gn_ref, dsc_ref, masks_ref, st_ref, o_ref, second_half, lane_sel):
    hg = u < HG_HEADS
    qc = P_QH + u * LANES if hg else P_QG + (u - HG_HEADS) * LANES
    kc = P_KH + u * LANES if hg else P_KG + (u - HG_HEADS) * LANES
    sl = slice(u * LANES, (u + 1) * LANES)
    q = pm_ref[rs, qc:qc + LANES]
    k = pm_ref[rs, kc:kc + LANES]
    heads = _unit_heads(u)

    def only(x, j):
        return x if j is None else jnp.where(lane_sel[j], x, 0.0)

    ws = []
    for li in range(len(LEVELS)):
        dl = dsc_ref[(2 + li) * CHUNK:(3 + li) * CHUNK, sl]
        ws.append(jnp.where(second_half[li], q, k) * jnp.exp(-jnp.abs(dl)))
    qk = q * k
    qhat = q * jnp.exp(dsc_ref[0:CHUNK, sl])
    khat = k * jnp.exp(dsc_ref[CHUNK:2 * CHUNK, sl])
    st = st_ref[u]
    stb = st.astype(BF16)
    upd = None
    for vc, gc, j in heads:
        att = None
        for li, w in enumerate(ws):
            wj = only(w, j).astype(BF16)
            x = masks_ref[li] * _dot_nt(wj, wj)
            att = x if att is None else att + x
        vb = v_ref[rs, vc:vc + LANES]
        dg = jnp.sum(only(qk, j), axis=1, keepdims=True)
        o = (_dot(att.astype(BF16), vb) + dg * vb.astype(F32)
             + _dot_nt(only(qhat, j).astype(BF16), stb))
        term = _dot_tn(vb, only(khat, j).astype(BF16))
        upd = term if upd is None else upd + term
        o_ref[rs, vc:vc + LANES] = (_rms(o, gn_ref[...]) * pm_ref[rs, gc:gc + LANES]).astype(o_ref.dtype)
    st_ref[u] = st * jnp.exp(dsc_ref[CHUNK - 1:CHUNK, sl]) + upd


def _recur_kernel(pm_ref, v_ref, gnh_ref, gng_ref, mats_ref, masks_ref,
                  o_ref, so_ref, st_ref, dsc_ref, *, n_chunks):
    s = pl.program_id(1)

    @pl.when(s == 0)
    def _():
        st_ref[...] = jnp.zeros_like(st_ref)

    rows = lax.broadcasted_iota(I32, (CHUNK, LANES), 0)
    lanes = lax.broadcasted_iota(I32, (CHUNK, LANES), 1)
    second_half = [(rows & c) != 0 for c in LEVELS]
    lane_sel = [lanes < GLA_DK, lanes >= GLA_DK]

    def chunk(ci, carry):
        r0 = pl.multiple_of(ci * CHUNK, CHUNK)
        rs = pl.ds(r0, CHUNK)
        g = jnp.concatenate([pm_ref[rs, P_LF:P_LF + 512], pm_ref[rs, P_LA:P_LA + 256]], axis=1)
        dsc_ref[...] = _dot(mats_ref[...], jnp.concatenate(_split3(g), axis=0))
        for u in range(N_UNITS):
            gn_ref = gnh_ref if u < HG_HEADS else gng_ref
            _recur_unit(u, rs, pm_ref, v_ref, gn_ref, dsc_ref, masks_ref, st_ref, o_ref, second_half, lane_sel)
        return carry

    lax.fori_loop(0, n_chunks, chunk, 0)

    @pl.when(s == pl.num_programs(1) - 1)
    def _():
        so_ref[...] = st_ref[...]


def _prompt_recurrence(pm, v, gn_h, gn_g, batch, seq, ts):
    mats, masks = _level_matrices()
    steps = seq // ts
    kern = functools.partial(_recur_kernel, n_chunks=ts // CHUNK)
    return pl.pallas_call(
        kern,
        grid=(batch, steps),
        in_specs=[
            pl.BlockSpec((ts, W_PREP), lambda b, s: (b * steps + s, 0)),
            pl.BlockSpec((ts, D_MODEL), lambda b, s: (b * steps + s, 0)),
            pl.BlockSpec((1, HG_DV), lambda b, s: (0, 0)),
            pl.BlockSpec((1, GLA_DV), lambda b, s: (0, 0)),
            pl.BlockSpec((N_MATS * CHUNK, 3 * CHUNK), lambda b, s: (0, 0)),
            pl.BlockSpec((len(LEVELS), CHUNK, CHUNK), lambda b, s: (0, 0, 0)),
        ],
        out_specs=[
            pl.BlockSpec((ts, D_MODEL), lambda b, s: (b * steps + s, 0)),
            pl.BlockSpec((None, N_UNITS, HG_DV, LANES), lambda b, s: (b, 0, 0, 0)),
        ],
        out_shape=[
            jax.ShapeDtypeStruct((batch * seq, D_MODEL), BF16),
            jax.ShapeDtypeStruct((batch, N_UNITS, HG_DV, LANES), F32),
        ],
        scratch_shapes=[
            pltpu.VMEM((N_UNITS, HG_DV, LANES), F32),
            pltpu.VMEM((N_MATS * CHUNK, N_DECAY), F32),
        ],
        compiler_params=_params(("arbitrary", "arbitrary")),
        name="prompt_recurrence",
    )(pm, v, gn_h, gn_g, jnp.asarray(mats, BF16), jnp.asarray(masks, F32))


def _unpack_prompt_states(st):
    b = st.shape[0]
    hg = jnp.swapaxes(st[:, :HG_HEADS], -1, -2)
    gl = st[:, HG_HEADS:].reshape(b, N_UNITS - HG_HEADS, GLA_DV, LANES // GLA_DK, GLA_DK)
    gl = gl.transpose(0, 1, 3, 4, 2).reshape(b, GLA_HEADS, GLA_DK, GLA_DV)
    return hg, gl


def _column(row, eye):
    return jnp.sum(jnp.where(eye, row, 0.0), axis=1, keepdims=True)


def _sample_kernel(pm_ref, v_ref, gnh_ref, gng_ref, sh_ref, sg_ref,
                   o_ref, nsh_ref, nsg_ref, oscr_ref, vscr_ref, ascr_ref, *, sb):
    vscr_ref[...] = v_ref[...].astype(F32)
    ascr_ref[:, 0:512] = jnp.exp(pm_ref[:, P_LF:P_LF + 512])
    ascr_ref[:, 512:768] = jnp.exp(pm_ref[:, P_LA:P_LA + 256])

    def eye(n):
        return lax.broadcasted_iota(I32, (n, n), 0) == lax.broadcasted_iota(I32, (n, n), 1)

    eye_h, eye_g = eye(HG_DK), eye(GLA_DK)

    def group(gi, carry):
        r8 = pl.ds(pl.multiple_of(gi * SUBLANES, SUBLANES), SUBLANES)
        a8 = ascr_ref[r8, :]
        kh8 = pm_ref[r8, P_KH:P_KH + 512]
        qh8 = pm_ref[r8, P_QH:P_QH + 512]
        kg8 = pm_ref[r8, P_KG:P_KG + 256]
        qg8 = pm_ref[r8, P_QG:P_QG + 256]
        v8 = vscr_ref[r8, :]
        o_rows = [[] for _ in range(HG_HEADS + GLA_HEADS)]
        for jj in range(SUBLANES):
            j = gi * SUBLANES + jj
            r = slice(jj, jj + 1)
            for h in range(HG_HEADS):
                sl = slice(h * HG_DK, (h + 1) * HG_DK)
                s_new = (_column(a8[r, sl], eye_h) * sh_ref[j, h]
                         + _column(kh8[r, sl], eye_h) * v8[r, h * HG_DV:(h + 1) * HG_DV])
                nsh_ref[j, h] = s_new
                o_rows[h].append(jnp.sum(_column(qh8[r, sl], eye_h) * s_new, axis=0, keepdims=True))
            for h in range(GLA_HEADS):
                sl = slice(h * GLA_DK, (h + 1) * GLA_DK)
                vc = HG_HEADS * HG_DV + h * GLA_DV
                s_new = (_column(a8[r, 512 + h * GLA_DK:512 + (h + 1) * GLA_DK], eye_g) * sg_ref[j, h]
                         + _column(kg8[r, sl], eye_g) * v8[r, vc:vc + GLA_DV])
                nsg_ref[j, h] = s_new
                o_rows[HG_HEADS + h].append(jnp.sum(_column(qg8[r, sl], eye_g) * s_new, axis=0, keepdims=True))
        for h in range(HG_HEADS + GLA_HEADS):
            oscr_ref[r8, h * LANES:(h + 1) * LANES] = jnp.concatenate(o_rows[h], axis=0)
        return carry

    lax.fori_loop(0, sb // SUBLANES, group, 0)
    for h in range(HG_HEADS + GLA_HEADS):
        sl = slice(h * LANES, (h + 1) * LANES)
        gn = gnh_ref[...] if h < HG_HEADS else gng_ref[...]
        g0 = P_GH + h * LANES if h < HG_HEADS else P_GG + (h - HG_HEADS) * LANES
        o_ref[:, sl] = (_rms(oscr_ref[:, sl], gn) * pm_ref[:, g0:g0 + LANES]).astype(o_ref.dtype)


def _sample_recurrence(pm, v, gn_h, gn_g, state_hgrn, state_gla, layer, sb):
    n = pm.shape[0]
    kern = functools.partial(_sample_kernel, sb=sb)
    hshape = (sb, HG_HEADS, HG_DK, HG_DV)
    gshape = (sb, GLA_HEADS, GLA_DK, GLA_DV)
    return pl.pallas_call(
        kern,
        grid=(n // sb,),
        in_specs=[
            pl.BlockSpec((sb, W_PREP), lambda i: (i, 0)),
            pl.BlockSpec((sb, D_MODEL), lambda i: (i, 0)),
            pl.BlockSpec((1, HG_DV), lambda i: (0, 0)),
            pl.BlockSpec((1, GLA_DV), lambda i: (0, 0)),
            pl.BlockSpec((None,) + hshape, lambda i: (layer, i, 0, 0, 0)),
            pl.BlockSpec((None,) + gshape, lambda i: (layer, i, 0, 0, 0)),
        ],
        out_specs=[
            pl.BlockSpec((sb, D_MODEL), lambda i: (i, 0)),
            pl.BlockSpec(hshape, lambda i: (i, 0, 0, 0)),
            pl.BlockSpec(gshape, lambda i: (i, 0, 0, 0)),
        ],
        out_shape=[
            jax.ShapeDtypeStruct((n, D_MODEL), BF16),
            jax.ShapeDtypeStruct(state_hgrn.shape[1:], F32),
            jax.ShapeDtypeStruct(state_gla.shape[1:], F32),
        ],
        scratch_shapes=[
            pltpu.VMEM((sb, D_MODEL), F32),
            pltpu.VMEM((sb, D_MODEL), F32),
            pltpu.VMEM((sb, N_DECAY), F32),
        ],
        compiler_params=_params(("arbitrary",)),
        name="sample_recurrence",
    )(pm, v, gn_h, gn_g, state_hgrn, state_gla)


def _dot3_rhs(x, w):
    xh, xm, xl = _split3(x)
    wh, wm, wl = _split3(w)
    return (_dot(xh, wh) + (_dot(xh, wm) + _dot(xm, wh))
            + (_dot(xh, wl) + _dot(xm, wm) + _dot(xl, wh)))


def _post_kernel(x_ref, o_ref, g1_ref, sh_ref, sc_ref, ng_ref, wo_ref, rw_ref, rb_ref,
                 x1_ref, h2_ref, ti_ref, tg_ref):
    x1 = x_ref[...] + g1_ref[...] * _dot(o_ref[...], wo_ref[...])
    x1_ref[...] = x1
    h2 = _rms(x1, ng_ref[...]) * (1.0 + sc_ref[...]) + sh_ref[...]
    h2_ref[...] = h2
    logits = _dot3_rhs(h2, rw_ref[...]) + rb_ref[...]
    lanes = lax.broadcasted_iota(I32, logits.shape, 1).astype(F32)
    vals = logits
    tops, idxs = [], []
    for _ in range(TOP_K):
        m = jnp.max(vals, axis=1, keepdims=True)
        idx = jnp.min(jnp.where(vals == m, lanes, float(N_EXPERTS)), axis=1, keepdims=True)
        tops.append(m)
        idxs.append(idx)
        vals = jnp.where(lanes == idx, -jnp.inf, vals)
    es = [jnp.exp(t - tops[0]) for t in tops]
    den = es[0] + es[1] + es[2] + es[3]
    ti_ref[...] = jnp.concatenate(idxs, axis=1).astype(I32)
    tg_ref[...] = jnp.concatenate([e / den for e in es], axis=1)


def _post(x2d, o2d, mod, per_row, rows_per_seq, norm_g, w_o, router_w, router_b, tm):
    t, d = x2d.shape
    if per_row:
        mod_spec = lambda c: pl.BlockSpec((tm, d), lambda i: (i, c))
    else:
        per = rows_per_seq // tm
        mod_spec = lambda c: pl.BlockSpec((None, 1, d), lambda i: (i // per, 0, c))
    return pl.pallas_call(
        _post_kernel,
        grid=(t // tm,),
        in_specs=[
            pl.BlockSpec((tm, d), lambda i: (i, 0)),
            pl.BlockSpec((tm, d), lambda i: (i, 0)),
            mod_spec(2), mod_spec(3), mod_spec(4),
            pl.BlockSpec((1, d), lambda i: (0, 0)),
            pl.BlockSpec((d, d), lambda i: (0, 0)),
            pl.BlockSpec((d, N_EXPERTS), lambda i: (0, 0)),
            pl.BlockSpec((1, N_EXPERTS), lambda i: (0, 0)),
        ],
        out_specs=[
            pl.BlockSpec((tm, d), lambda i: (i, 0)),
            pl.BlockSpec((tm, d), lambda i: (i, 0)),
            pl.BlockSpec((tm, TOP_K), lambda i: (i, 0)),
            pl.BlockSpec((tm, TOP_K), lambda i: (i, 0)),
        ],
        out_shape=[
            jax.ShapeDtypeStruct((t, d), F32),
            jax.ShapeDtypeStruct((t, d), F32),
            jax.ShapeDtypeStruct((t, TOP_K), I32),
            jax.ShapeDtypeStruct((t, TOP_K), F32),
        ],
        compiler_params=_params(("arbitrary",)),
        name="post_mixer",
    )(x2d, o2d, mod, mod, mod, norm_g, w_o, router_w, router_b)


def _dot3_lhs(x, m_bf16):
    hi, mid, lo = _split3(x)
    return _dot(hi, m_bf16) + _dot(mid, m_bf16) + _dot(lo, m_bf16)


def _route_kernel(ti_ref, dest_ref, be_ref, nu_ref, *, n_tok_blocks, tb, n_row_blocks):
    lanes = lax.broadcasted_iota(I32, (tb, N_EXPERTS), 1)

    def onehots(i):
        r0 = pl.multiple_of(i * tb, tb)
        ti = ti_ref[pl.ds(r0, tb), :]
        return [(ti[:, k:k + 1] == lanes).astype(F32) for k in range(TOP_K)]

    def count(i, acc):
        oh = onehots(i)
        return acc + jnp.sum(oh[0] + oh[1] + oh[2] + oh[3], axis=0, keepdims=True)

    counts = lax.fori_loop(0, n_tok_blocks, count, jnp.zeros((1, N_EXPERTS), F32))
    blocks = jnp.floor((counts + (EXPERT_BLOCK - 1)) * (1.0 / EXPERT_BLOCK))
    padded = blocks * EXPERT_BLOCK
    e_r = lax.broadcasted_iota(I32, (N_EXPERTS, N_EXPERTS), 0)
    e_c = lax.broadcasted_iota(I32, (N_EXPERTS, N_EXPERTS), 1)
    upper = (e_r <= e_c).astype(BF16)
    padded_end = _dot3_lhs(padded, upper)
    padded_start = padded_end - padded

    t_r = lax.broadcasted_iota(I32, (tb, tb), 0)
    t_c = lax.broadcasted_iota(I32, (tb, tb), 1)
    strict = (t_c < t_r).astype(BF16)

    def place(i, carry):
        oh = onehots(i)
        c = oh[0] + oh[1] + oh[2] + oh[3]
        rank = _dot(strict, c.astype(BF16)) + carry
        cols = [jnp.sum(oh[k] * rank, axis=1, keepdims=True) for k in range(TOP_K)]
        r0 = pl.multiple_of(i * tb, tb)
        dest_ref[pl.ds(r0, tb), :] = jnp.concatenate(cols, axis=1).astype(I32)
        return carry + jnp.sum(c, axis=0, keepdims=True)

    lax.fori_loop(0, n_tok_blocks, place, padded_start)

    starts = (lax.broadcasted_iota(I32, (n_row_blocks, N_EXPERTS), 0) * EXPERT_BLOCK).astype(F32)
    be = jnp.sum((starts >= padded_end).astype(F32), axis=1, keepdims=True)
    be_ref[...] = jnp.minimum(be, N_EXPERTS - 1.0).astype(I32)
    nu_ref[...] = (padded_end[:, N_EXPERTS - 1:N_EXPERTS] * (1.0 / EXPERT_BLOCK)).astype(I32)


def _route(top_idx, n_row_blocks):
    t = top_idx.shape[0]
    tb = 128
    nbp = -(-n_row_blocks // 8) * 8
    kern = functools.partial(_route_kernel, n_tok_blocks=t // tb, tb=tb, n_row_blocks=nbp)
    dest, be, nu = pl.pallas_call(
        kern,
        out_shape=[
            jax.ShapeDtypeStruct((t, TOP_K), I32),
            jax.ShapeDtypeStruct((nbp, 1), I32),
            jax.ShapeDtypeStruct((1, 1), I32),
        ],
        compiler_params=_params(),
        name="route",
    )(top_idx)
    return dest, be[:n_row_blocks, 0], nu[0]


def _dispatch_kernel(dest_ref, h_ref, xin_ref, xb_ref, sem, *, tb):
    del xin_ref
    base = pl.program_id(0) * (tb * TOP_K)

    def row_copy(t, d):
        return pltpu.make_async_copy(h_ref.at[pl.ds(t, 1)], xb_ref.at[pl.ds(d, 1)], sem)

    def start(t, carry):
        for k in range(TOP_K):
            row_copy(t, dest_ref[base + t * TOP_K + k]).start()
        return carry

    def wait(t, carry):
        for k in range(TOP_K):
            row_copy(0, 0).wait()
        return carry

    lax.fori_loop(0, tb, start, 0)
    lax.fori_loop(0, tb, wait, 0)


def _dispatch(dest_flat, h2, xb, tb):
    t, d = h2.shape
    kern = functools.partial(_dispatch_kernel, tb=tb)
    return pl.pallas_call(
        kern,
        grid_spec=pltpu.PrefetchScalarGridSpec(
            num_scalar_prefetch=1,
            grid=(t // tb,),
            in_specs=[
                pl.BlockSpec((tb, d), lambda i, dest: (i, 0)),
                pl.BlockSpec(memory_space=pl.ANY),
            ],
            out_specs=pl.BlockSpec(memory_space=pl.ANY),
            scratch_shapes=[pltpu.SemaphoreType.DMA(())],
        ),
        out_shape=jax.ShapeDtypeStruct(xb.shape, xb.dtype),
        input_output_aliases={2: 0},
        compiler_params=_params(("arbitrary",)),
        name="dispatch",
    )(dest_flat, h2, xb)


def _expert_kernel(be_ref, nu_ref, x_ref, wu_ref, bu_ref, wd_ref, bd_ref, y_ref, wu_bf, wd_bf):
    i = pl.program_id(0)
    e = be_ref[i]
    prev = be_ref[jnp.maximum(i - 1, 0)]

    @pl.when((i == 0) | (e != prev))
    def _():
        wu_bf[...] = wu_ref[...].astype(BF16)
        wd_bf[...] = wd_ref[...].astype(BF16)

    @pl.when(i < nu_ref[0])
    def _():
        u = _dot(x_ref[...].astype(BF16), wu_bf[...]) + bu_ref[...]
        glu = jnp.minimum(u[:, :D_FF], SWIGLU_LIMIT)
        lin = jnp.clip(u[:, D_FF:], -SWIGLU_LIMIT, SWIGLU_LIMIT)
        act = glu * _sigmoid(SWIGLU_ALPHA * glu) * (lin + 1.0)
        y_ref[...] = _dot(act.astype(BF16), wd_bf[...]) + bd_ref[...]

    @pl.when(i >= nu_ref[0])
    def _():
        y_ref[...] = jnp.zeros_like(y_ref)


def _experts(block_e, n_used, xb, layer, w_up, b_up, w_down, b_down):
    rows, d = xb.shape
    nb = rows // EXPERT_BLOCK
    return pl.pallas_call(
        _expert_kernel,
        grid_spec=pltpu.PrefetchScalarGridSpec(
            num_scalar_prefetch=2,
            grid=(nb,),
            in_specs=[
                pl.BlockSpec((EXPERT_BLOCK, d), lambda i, be, nu: (i, 0)),
                pl.BlockSpec((None, None, d, 2 * D_FF), lambda i, be, nu: (layer, be[i], 0, 0)),
                pl.BlockSpec((None, None, 1, 2 * D_FF), lambda i, be, nu: (layer, be[i], 0, 0)),
                pl.BlockSpec((None, None, D_FF, d), lambda i, be, nu: (layer, be[i], 0, 0)),
                pl.BlockSpec((None, None, 1, d), lambda i, be, nu: (layer, be[i], 0, 0)),
            ],
            out_specs=pl.BlockSpec((EXPERT_BLOCK, d), lambda i, be, nu: (i, 0)),
            scratch_shapes=[
                pltpu.VMEM((d, 2 * D_FF), BF16),
                pltpu.VMEM((D_FF, d), BF16),
            ],
        ),
        out_shape=jax.ShapeDtypeStruct((rows, d), F32),
        compiler_params=_params(("arbitrary",)),
        name="experts",
    )(block_e, n_used, xb, w_up, b_up, w_down, b_down)


def _combine_kernel(dest_ref, yb_ref, tg_ref, x1_ref, g2_ref, fg_ref, out_ref, buf, sem, *, tb, final):
    base = pl.program_id(0) * (tb * TOP_K)

    def row_copy(t, k, d):
        return pltpu.make_async_copy(yb_ref.at[pl.ds(d, 1)], buf.at[k, pl.ds(t, 1)], sem)

    def start(t, carry):
        for k in range(TOP_K):
            row_copy(t, k, dest_ref[base + t * TOP_K + k]).start()
        return carry

    def wait(t, carry):
        for k in range(TOP_K):
            row_copy(0, k, 0).wait()
        return carry

    lax.fori_loop(0, tb, start, 0)
    lax.fori_loop(0, tb, wait, 0)
    tg = tg_ref[...]
    y = tg[:, 0:1] * buf[0]
    for k in range(1, TOP_K):
        y = y + tg[:, k:k + 1] * buf[k]
    x2 = x1_ref[...] + g2_ref[...] * y
    out_ref[...] = _rms(x2, fg_ref[...]) if final else x2


def _combine(dest_flat, yb, gate, x1, mod, per_row, rows_per_seq, final_g, final, tb):
    t, d = x1.shape
    if per_row:
        mod_spec = pl.BlockSpec((tb, d), lambda i, dest: (i, 5))
    else:
        per = rows_per_seq // tb
        mod_spec = pl.BlockSpec((None, 1, d), lambda i, dest: (i // per, 0, 5))
    kern = functools.partial(_combine_kernel, tb=tb, final=final)
    return pl.pallas_call(
        kern,
        grid_spec=pltpu.PrefetchScalarGridSpec(
            num_scalar_prefetch=1,
            grid=(t // tb,),
            in_specs=[
                pl.BlockSpec(memory_space=pl.ANY),
                pl.BlockSpec((tb, TOP_K), lambda i, dest: (i, 0)),
                pl.BlockSpec((tb, d), lambda i, dest: (i, 0)),
                mod_spec,
                pl.BlockSpec((1, d), lambda i, dest: (0, 0)),
            ],
            out_specs=pl.BlockSpec((tb, d), lambda i, dest: (i, 0)),
            scratch_shapes=[
                pltpu.VMEM((TOP_K, tb, d), F32),
                pltpu.SemaphoreType.DMA(()),
            ],
        ),
        out_shape=jax.ShapeDtypeStruct((t, d), F32),
        compiler_params=_params(("arbitrary",)),
        name="combine",
    )(dest_flat, yb, gate, x1, mod, final_g)


def kernel(x_prompt, x_sample, state_hgrn, state_gla, c_prompt, c_sample, hgrn_lower_bounds, norm1_g, norm2_g, w_ada, b_ada, w_in, w_gk2, b_gk2, gn_hgrn, gn_gla, w_o, router_w, router_b, w_up, b_up, w_down, b_down, final_norm_g):
    depth = w_in.shape[0]
    batch, seq, d = x_prompt.shape
    n_s = x_sample.shape[0]
    assert x_sample.shape[1] == 1 and d == D_MODEL
    t_p = batch * seq
    tm_p = min(512, seq)
    tm_s = n_s
    tb = 128
    sb = 16
    assert seq % tm_p == 0 and seq % tb == 0 and n_s % tb == 0 and n_s % sb == 0

    lbs = jnp.cumsum(jax.nn.softmax(hgrn_lower_bounds.astype(F32), axis=0), axis=0)
    lbs = lbs - lbs[0:1]

    mod = _modulation(jnp.concatenate([c_prompt, c_sample], axis=0), w_ada, b_ada)
    w_main = w_in[:, :, :W_MAIN].astype(BF16)
    w_rank = w_in[:, :, W_MAIN:].astype(BF16)
    w_gk2b = w_gk2.astype(BF16)
    w_ob = w_o.astype(BF16)
    b_up4 = b_up.reshape(depth, N_EXPERTS, 1, 2 * D_FF)
    b_down4 = b_down.reshape(depth, N_EXPERTS, 1, d)

    m_slots = (t_p + n_s) * TOP_K
    n_row_blocks = -(-m_slots // EXPERT_BLOCK) + N_EXPERTS
    rows = n_row_blocks * EXPERT_BLOCK

    xp = x_prompt.reshape(t_p, d)
    xs = x_sample.reshape(n_s, d)
    hg_p, gl_p, hg_s, gl_s = [], [], [], []
    for l in range(depth):
        mod_p = mod[l, :batch].reshape(batch, 1, 6 * d)
        mod_s = mod[l, batch:]
        lb = lbs[l].reshape(1, -1)
        n1 = norm1_g[l].reshape(1, d)
        n2 = norm2_g[l].reshape(1, d)
        gnh = gn_hgrn[l].reshape(1, -1)
        gng = gn_gla[l].reshape(1, -1)
        bg = b_gk2[l].reshape(1, -1)
        rb = router_b[l].reshape(1, -1)
        final = l == depth - 1
        fg = final_norm_g.reshape(1, d)

        pm_p, v_p = _in_proj(xp, mod_p, False, seq, n1, lb, l, w_main, w_rank, w_gk2b[l], bg, tm_p)
        pm_s, v_s = _in_proj(xs, mod_s, True, 1, n1, lb, l, w_main, w_rank, w_gk2b[l], bg, tm_s)

        o_p, st_p = _prompt_recurrence(pm_p, v_p, gnh, gng, batch, seq, tm_p)
        o_s, sh_s, sg_s = _sample_recurrence(pm_s, v_s, gnh, gng, state_hgrn, state_gla, l, sb)
        sh_p, sg_p = _unpack_prompt_states(st_p)
        hg_p.append(sh_p)
        gl_p.append(sg_p)
        hg_s.append(sh_s)
        gl_s.append(sg_s)

        x1_p, h2_p, ti_p, tg_p = _post(xp, o_p, mod_p, False, seq, n2, w_ob[l], router_w[l], rb, tm_p)
        x1_s, h2_s, ti_s, tg_s = _post(xs, o_s, mod_s, True, 1, n2, w_ob[l], router_w[l], rb, tm_s)

        dest, block_e, n_used = _route(jnp.concatenate([ti_p, ti_s], axis=0), n_row_blocks)
        dest_p = dest[:t_p].reshape(-1)
        dest_s = dest[t_p:].reshape(-1)
        xb = jnp.zeros((rows, d), F32)
        xb = _dispatch(dest_p, h2_p, xb, tb)
        xb = _dispatch(dest_s, h2_s, xb, tb)
        yb = _experts(block_e, n_used, xb, l, w_up, b_up4, w_down, b_down4)
        xp = _combine(dest_p, yb, tg_p, x1_p, mod_p, False, seq, fg, final, tb)
        xs = _combine(dest_s, yb, tg_s, x1_s, mod_s, True, 1, fg, final, tb)

    return (xp.reshape(batch, seq, d), xs.reshape(n_s, 1, d),
            jnp.stack(hg_p), jnp.stack(gl_p), jnp.stack(hg_s), jnp.stack(gl_s))
```

```python
import functools

import numpy as np
import jax
import jax.numpy as jnp
from jax import lax
from jax.experimental import pallas as pl
from jax.experimental.pallas import tpu as pltpu

F32 = jnp.float32
BF16 = jnp.bfloat16
I32 = jnp.int32

D_MODEL = 1024
HG_HEADS, HG_DK, HG_DV = 4, 128, 128
GLA_HEADS, GLA_DK, GLA_DV = 4, 64, 128
GLA_GATE_RANK = 16
GLA_GATE_NORM = 16.0
CHUNK = 64
N_EXPERTS = 32
TOP_K = 4
D_FF = 1024
SWIGLU_LIMIT = 7.0
SWIGLU_ALPHA = 1.702
EPS = 1e-6
LANES = 128
SUBLANES = 8
SMEM_WORDS = 1024

HQ0, HF0, HI0, HGATE0 = 0, 512, 1024, 1536
GQ0, GK0, GV0, GGATE0 = 2048, 2304, 2560, 3072
W_MAIN = 3584
P_QH, P_LF, P_KH, P_GH, P_QG, P_KG, P_GG, P_LA = 0, 512, 1024, 1536, 2048, 2304, 2560, 3072
W_PREP = 3328
N_DECAY = HG_HEADS * HG_DK + GLA_HEADS * GLA_DK
N_UNITS = N_DECAY // LANES

LEVELS = (32, 16, 8, 4, 2, 1)
N_MATS = 2 + len(LEVELS)

EXPERT_BLOCK = 512
VMEM_LIMIT = 56 * 1024 * 1024


def _sigmoid(x):
    return 1.0 / (1.0 + jnp.exp(-x))


def _silu(x):
    return x * _sigmoid(x)


def _log_sigmoid(x):
    return jnp.minimum(x, 0.0) - jnp.log1p(jnp.exp(-jnp.abs(x)))


def _rms(x, g):
    return x * lax.rsqrt(jnp.mean(x * x, axis=-1, keepdims=True) + EPS) * g


def _dot(a, b):
    return jnp.dot(a, b, preferred_element_type=F32)


def _dot_nt(a, b):
    return lax.dot_general(a, b, (((1,), (1,)), ((), ())), preferred_element_type=F32)


def _dot_tn(a, b):
    return lax.dot_general(a, b, (((0,), (0,)), ((), ())), preferred_element_type=F32)


def _split3(x):
    hi = x.astype(BF16)
    r = x - hi.astype(F32)
    mid = r.astype(BF16)
    lo = (r - mid.astype(F32)).astype(BF16)
    return hi, mid, lo


def _params(sem=None):
    return pltpu.CompilerParams(dimension_semantics=sem, vmem_limit_bytes=VMEM_LIMIT)


def _mod_kernel(c_ref, w_ref, b_ref, o_ref):
    s = _silu(c_ref[...]).astype(BF16)
    o_ref[...] = _dot(s, w_ref[...].astype(BF16)) + b_ref[...]


def _modulation(c_all, w_ada, b_ada):
    depth, d, n = w_ada.shape
    rows = c_all.shape[0]
    tn = 1536
    return pl.pallas_call(
        _mod_kernel,
        grid=(depth, n // tn),
        in_specs=[
            pl.BlockSpec((rows, d), lambda l, j: (0, 0)),
            pl.BlockSpec((None, d, tn), lambda l, j: (l, 0, j)),
            pl.BlockSpec((None, 1, tn), lambda l, j: (l, 0, j)),
        ],
        out_specs=pl.BlockSpec((None, rows, tn), lambda l, j: (l, 0, j)),
        out_shape=jax.ShapeDtypeStruct((depth, rows, n), F32),
        compiler_params=_params(("arbitrary", "arbitrary")),
        name="ada_mod",
    )(c_all, w_ada, b_ada.reshape(depth, 1, n))


def _hgrn_log_f(z, lb):
    a1 = jnp.log(lb)
    a2 = jnp.log1p(-lb) + _log_sigmoid(z)
    mx = jnp.maximum(a1, a2)
    mn = jnp.minimum(a1, a2)
    return mx + jnp.log1p(jnp.exp(mn - mx))


def _proj_kernel(x_ref, sh_ref, sc_ref, g_ref, lb_ref, w_ref, wr_ref, wg_ref, bg_ref, pm_ref, v_ref):
    h = _rms(x_ref[...], g_ref[...]) * (1.0 + sc_ref[...]) + sh_ref[...]
    hb = h.astype(BF16)
    lb = lb_ref[...]

    def seg(c0, n):
        return _dot(hb, w_ref[:, c0:c0 + n])

    pm_ref[:, P_QH:P_QH + 512] = _silu(seg(HQ0, 512))
    z = seg(HF0, 512)
    pm_ref[:, P_LF:P_LF + 512] = _hgrn_log_f(z, lb)
    pm_ref[:, P_KH:P_KH + 512] = (1.0 - lb) * _sigmoid(-z)
    v_ref[:, 0:512] = seg(HI0, 512).astype(BF16)
    pm_ref[:, P_GH:P_GH + 512] = _silu(seg(HGATE0, 512))
    pm_ref[:, P_QG:P_QG + 256] = seg(GQ0, 256) * (GLA_DK ** -0.5)
    pm_ref[:, P_KG:P_KG + 256] = seg(GK0, 256)
    v_ref[:, 512:1024] = seg(GV0, 512).astype(BF16)
    pm_ref[:, P_GG:P_GG + 512] = _silu(seg(GGATE0, 512))
    gr = _dot(hb, wr_ref[...])
    glog = _dot(gr.astype(BF16), wg_ref[...]) + bg_ref[...]
    pm_ref[:, P_LA:P_LA + 256] = _log_sigmoid(glog) / GLA_GATE_NORM


def _in_proj(x2d, mod, per_row, rows_per_seq, norm_g, lb, layer, w_main, w_rank, w_gk2, b_gk2, tm):
    t, d = x2d.shape
    if per_row:
        mod_spec = lambda c: pl.BlockSpec((tm, d), lambda i: (i, c))
    else:
        per = rows_per_seq // tm
        mod_spec = lambda c: pl.BlockSpec((None, 1, d), lambda i: (i // per, 0, c))
    ngl = w_gk2.shape[1]
    return pl.pallas_call(
        _proj_kernel,
        grid=(t // tm,),
        in_specs=[
            pl.BlockSpec((tm, d), lambda i: (i, 0)),
            mod_spec(0), mod_spec(1),
            pl.BlockSpec((1, d), lambda i: (0, 0)),
            pl.BlockSpec((1, HG_HEADS * HG_DK), lambda i: (0, 0)),
            pl.BlockSpec((None, d, W_MAIN), lambda i: (layer, 0, 0)),
            pl.BlockSpec((None, d, GLA_GATE_RANK), lambda i: (layer, 0, 0)),
            pl.BlockSpec((GLA_GATE_RANK, ngl), lambda i: (0, 0)),
            pl.BlockSpec((1, ngl), lambda i: (0, 0)),
        ],
        out_specs=[
            pl.BlockSpec((tm, W_PREP), lambda i: (i, 0)),
            pl.BlockSpec((tm, D_MODEL), lambda i: (i, 0)),
        ],
        out_shape=[
            jax.ShapeDtypeStruct((t, W_PREP), F32),
            jax.ShapeDtypeStruct((t, D_MODEL), BF16),
        ],
        compiler_params=_params(("arbitrary",)),
        name="in_proj",
    )(x2d, mod, mod, norm_g, lb, w_main, w_rank, w_gk2, b_gk2)


def _level_matrices():
    t = np.arange(CHUNK)[:, None]
    u = np.arange(CHUNK)[None, :]
    tri = (u <= t).astype(np.float32)
    mats = [tri, (u > t).astype(np.float32)]
    masks = []
    for c in LEVELS:
        ref = (t // (2 * c)) * (2 * c) + c - 1
        mats.append(tri - (u <= ref).astype(np.float32))
        masks.append(((t // (2 * c) == u // (2 * c)) & (t % (2 * c) >= c) & (u % (2 * c) < c)).astype(np.float32))
    m = np.concatenate(mats, 0)
    return np.concatenate([m, m, m], 1), np.stack(masks, 0)


def _unit_heads(u):
    if u < HG_HEADS:
        return [(u * HG_DV, P_GH + u * HG_DV, None)]
    heads = []
    for j in range(LANES // GLA_DK):
        h = (u - HG_HEADS) * (LANES // GLA_DK) + j
        heads.append((HG_HEADS * HG_DV + h * GLA_DV, P_GG + h * GLA_DV, j))
    return heads


def _recur_unit(u, rs, pm_ref, v_ref, gn_ref, dsc_ref, masks_ref, st_ref, o_ref, second_half, lane_sel):
    hg = u < HG_HEADS
    qc = P_QH + u * LANES if hg else P_QG + (u - HG_HEADS) * LANES
    kc = P_KH + u * LANES if hg else P_KG + (u - HG_HEADS) * LANES
    sl = slice(u * LANES, (u + 1) * LANES)
    q = pm_ref[rs, qc:qc + LANES]
    k = pm_ref[rs, kc:kc + LANES]
    heads = _unit_heads(u)

    def only(x, j):
        return x if j is None else jnp.where(lane_sel[j], x, 0.0)

    ws = []
    for li in range(len(LEVELS)):
        dl = dsc_ref[(2 + li) * CHUNK:(3 + li) * CHUNK, sl]
        ws.append(jnp.where(second_half[li], q, k) * jnp.exp(-jnp.abs(dl)))
    qk = q * k
    qhat = q * jnp.exp(dsc_ref[0:CHUNK, sl])
    khat = k * jnp.exp(dsc_ref[CHUNK:2 * CHUNK, sl])
    st = st_ref[u]
    stb = st.astype(BF16)
    upd = None
    for vc, gc, j in heads:
        att = None
        for li, w in enumerate(ws):
            wj = only(w, j).astype(BF16)
            x = masks_ref[li] * _dot_nt(wj, wj)
            att = x if att is None else att + x
        vb = v_ref[rs, vc:vc + LANES]
        dg = jnp.sum(only(qk, j), axis=1, keepdims=True)
        o = (_dot(att.astype(BF16), vb) + dg * vb.astype(F32)
             + _dot_nt(only(qhat, j).astype(BF16), stb))
        term = _dot_tn(vb, only(khat, j).astype(BF16))
        upd = term if upd is None else upd + term
        o_ref[rs, vc:vc + LANES] = (_rms(o, gn_ref[...]) * pm_ref[rs, gc:gc + LANES]).astype(o_ref.dtype)
    st_ref[u] = st * jnp.exp(dsc_ref[CHUNK - 1:CHUNK, sl]) + upd


def _recur_kernel(pm_ref, v_ref, gnh_ref, gng_ref, mats_ref, masks_ref,
                  o_ref, so_ref, st_ref, dsc_ref, *, n_chunks):
    s = pl.program_id(1)

    @pl.when(s == 0)
    def _():
        st_ref[...] = jnp.zeros_like(st_ref)

    rows = lax.broadcasted_iota(I32, (CHUNK, LANES), 0)
    lanes = lax.broadcasted_iota(I32, (CHUNK, LANES), 1)
    second_half = [(rows & c) != 0 for c in LEVELS]
    lane_sel = [lanes < GLA_DK, lanes >= GLA_DK]

    def chunk(ci, carry):
        r0 = pl.multiple_of(ci * CHUNK, CHUNK)
        rs = pl.ds(r0, CHUNK)
        g = jnp.concatenate([pm_ref[rs, P_LF:P_LF + 512], pm_ref[rs, P_LA:P_LA + 256]], axis=1)
        dsc_ref[...] = _dot(mats_ref[...], jnp.concatenate(_split3(g), axis=0))
        for u in range(N_UNITS):
            gn_ref = gnh_ref if u < HG_HEADS else gng_ref
            _recur_unit(u, rs, pm_ref, v_ref, gn_ref, dsc_ref, masks_ref, st_ref, o_ref, second_half, lane_sel)
        return carry

    lax.fori_loop(0, n_chunks, chunk, 0)

    @pl.when(s == pl.num_programs(1) - 1)
    def _():
        so_ref[...] = st_ref[...]


def _prompt_recurrence(pm, v, gn_h, gn_g, batch, seq, ts):
    mats, masks = _level_matrices()
    steps = seq // ts
    kern = functools.partial(_recur_kernel, n_chunks=ts // CHUNK)
    return pl.pallas_call(
        kern,
        grid=(batch, steps),
        in_specs=[
            pl.BlockSpec((ts, W_PREP), lambda b, s: (b * steps + s, 0)),
            pl.BlockSpec((ts, D_MODEL), lambda b, s: (b * steps + s, 0)),
            pl.BlockSpec((1, HG_DV), lambda b, s: (0, 0)),
            pl.BlockSpec((1, GLA_DV), lambda b, s: (0, 0)),
            pl.BlockSpec((N_MATS * CHUNK, 3 * CHUNK), lambda b, s: (0, 0)),
            pl.BlockSpec((len(LEVELS), CHUNK, CHUNK), lambda b, s: (0, 0, 0)),
        ],
        out_specs=[
            pl.BlockSpec((ts, D_MODEL), lambda b, s: (b * steps + s, 0)),
            pl.BlockSpec((None, N_UNITS, HG_DV, LANES), lambda b, s: (b, 0, 0, 0)),
        ],
        out_shape=[
            jax.ShapeDtypeStruct((batch * seq, D_MODEL), BF16),
            jax.ShapeDtypeStruct((batch, N_UNITS, HG_DV, LANES), F32),
        ],
        scratch_shapes=[
            pltpu.VMEM((N_UNITS, HG_DV, LANES), F32),
            pltpu.VMEM((N_MATS * CHUNK, N_DECAY), F32),
        ],
        compiler_params=_params(("arbitrary", "arbitrary")),
        name="prompt_recurrence",
    )(pm, v, gn_h, gn_g, jnp.asarray(mats, BF16), jnp.asarray(masks, F32))


def _unpack_prompt_states(st):
    b = st.shape[0]
    hg = jnp.swapaxes(st[:, :HG_HEADS], -1, -2)
    gl = st[:, HG_HEADS:].reshape(b, N_UNITS - HG_HEADS, GLA_DV, LANES // GLA_DK, GLA_DK)
    gl = gl.transpose(0, 1, 3, 4, 2).reshape(b, GLA_HEADS, GLA_DK, GLA_DV)
    return hg, gl


def _sample_kernel(pm_ref, v_ref, gnh_ref, gng_ref, sh_ref, sg_ref,
                   o_ref, nsh_ref, nsg_ref, oscr_ref, vscr_ref, ascr_ref, *, sb):
    vscr_ref[...] = v_ref[...].astype(F32)
    ascr_ref[:, 0:512] = jnp.exp(pm_ref[:, P_LF:P_LF + 512])
    ascr_ref[:, 512:768] = jnp.exp(pm_ref[:, P_LA:P_LA + 256])

    def group(gi, carry):
        r8 = pl.ds(pl.multiple_of(gi * SUBLANES, SUBLANES), SUBLANES)
        a8 = ascr_ref[r8, :]
        kh8 = pm_ref[r8, P_KH:P_KH + 512]
        qh8 = pm_ref[r8, P_QH:P_QH + 512]
        kg8 = pm_ref[r8, P_KG:P_KG + 256]
        qg8 = pm_ref[r8, P_QG:P_QG + 256]
        v8 = vscr_ref[r8, :]
        o_rows = [[] for _ in range(HG_HEADS + GLA_HEADS)]
        for h in range(HG_HEADS):
            sl = slice(h * HG_DK, (h + 1) * HG_DK)
            at, kt, qt = a8[:, sl].T, kh8[:, sl].T, qh8[:, sl].T
            for jj in range(SUBLANES):
                j = gi * SUBLANES + jj
                c = slice(jj, jj + 1)
                s_new = at[:, c] * sh_ref[j, h] + kt[:, c] * v8[c, h * HG_DV:(h + 1) * HG_DV]
                nsh_ref[j, h] = s_new
                o_rows[h].append(jnp.sum(qt[:, c] * s_new, axis=0, keepdims=True))
        for h in range(GLA_HEADS):
            sl = slice(h * GLA_DK, (h + 1) * GLA_DK)
            vc = HG_HEADS * HG_DV + h * GLA_DV
            at, kt, qt = a8[:, 512 + h * GLA_DK:512 + (h + 1) * GLA_DK].T, kg8[:, sl].T, qg8[:, sl].T
            for jj in range(SUBLANES):
                j = gi * SUBLANES + jj
                c = slice(jj, jj + 1)
                s_new = at[:, c] * sg_ref[j, h] + kt[:, c] * v8[c, vc:vc + GLA_DV]
                nsg_ref[j, h] = s_new
                o_rows[HG_HEADS + h].append(jnp.sum(qt[:, c] * s_new, axis=0, keepdims=True))
        for h in range(HG_HEADS + GLA_HEADS):
            oscr_ref[r8, h * LANES:(h + 1) * LANES] = jnp.concatenate(o_rows[h], axis=0)
        return carry

    lax.fori_loop(0, sb // SUBLANES, group, 0)
    for h in range(HG_HEADS + GLA_HEADS):
        sl = slice(h * LANES, (h + 1) * LANES)
        gn = gnh_ref[...] if h < HG_HEADS else gng_ref[...]
        g0 = P_GH + h * LANES if h < HG_HEADS else P_GG + (h - HG_HEADS) * LANES
        o_ref[:, sl] = (_rms(oscr_ref[:, sl], gn) * pm_ref[:, g0:g0 + LANES]).astype(o_ref.dtype)


def _sample_recurrence(pm, v, gn_h, gn_g, state_hgrn, state_gla, layer, sb):
    n = pm.shape[0]
    kern = functools.partial(_sample_kernel, sb=sb)
    hshape = (sb, HG_HEADS, HG_DK, HG_DV)
    gshape = (sb, GLA_HEADS, GLA_DK, GLA_DV)
    return pl.pallas_call(
        kern,
        grid=(n // sb,),
        in_specs=[
            pl.BlockSpec((sb, W_PREP), lambda i: (i, 0)),
            pl.BlockSpec((sb, D_MODEL), lambda i: (i, 0)),
            pl.BlockSpec((1, HG_DV), lambda i: (0, 0)),
            pl.BlockSpec((1, GLA_DV), lambda i: (0, 0)),
            pl.BlockSpec((None,) + hshape, lambda i: (layer, i, 0, 0, 0)),
            pl.BlockSpec((None,) + gshape, lambda i: (layer, i, 0, 0, 0)),
        ],
        out_specs=[
            pl.BlockSpec((sb, D_MODEL), lambda i: (i, 0)),
            pl.BlockSpec(hshape, lambda i: (i, 0, 0, 0)),
            pl.BlockSpec(gshape, lambda i: (i, 0, 0, 0)),
        ],
        out_shape=[
            jax.ShapeDtypeStruct((n, D_MODEL), BF16),
            jax.ShapeDtypeStruct(state_hgrn.shape[1:], F32),
            jax.ShapeDtypeStruct(state_gla.shape[1:], F32),
        ],
        scratch_shapes=[
            pltpu.VMEM((sb, D_MODEL), F32),
            pltpu.VMEM((sb, D_MODEL), F32),
            pltpu.VMEM((sb, N_DECAY), F32),
        ],
        compiler_params=_params(("arbitrary",)),
        name="sample_recurrence",
    )(pm, v, gn_h, gn_g, state_hgrn, state_gla)


def _dot3_rhs(x, w):
    xh, xm, xl = _split3(x)
    wh, wm, wl = _split3(w)
    return (_dot(xh, wh) + (_dot(xh, wm) + _dot(xm, wh))
            + (_dot(xh, wl) + _dot(xm, wm) + _dot(xl, wh)))


def _post_kernel(x_ref, o_ref, g1_ref, sh_ref, sc_ref, ng_ref, wo_ref, rw_ref, rb_ref,
                 x1_ref, h2_ref, ti_ref, tg_ref):
    x1 = x_ref[...] + g1_ref[...] * _dot(o_ref[...], wo_ref[...])
    x1_ref[...] = x1
    h2 = _rms(x1, ng_ref[...]) * (1.0 + sc_ref[...]) + sh_ref[...]
    h2_ref[...] = h2
    logits = _dot3_rhs(h2, rw_ref[...]) + rb_ref[...]
    lanes = lax.broadcasted_iota(I32, logits.shape, 1).astype(F32)
    vals = logits
    tops, idxs = [], []
    for _ in range(TOP_K):
        m = jnp.max(vals, axis=1, keepdims=True)
        idx = jnp.min(jnp.where(vals == m, lanes, float(N_EXPERTS)), axis=1, keepdims=True)
        tops.append(m)
        idxs.append(idx)
        vals = jnp.where(lanes == idx, -jnp.inf, vals)
    es = [jnp.exp(t - tops[0]) for t in tops]
    den = es[0] + es[1] + es[2] + es[3]
    ti_ref[...] = jnp.concatenate(idxs, axis=1).astype(I32)
    tg_ref[...] = jnp.concatenate([e / den for e in es], axis=1)


def _post(x2d, o2d, mod, per_row, rows_per_seq, norm_g, w_o, router_w, router_b, tm):
    t, d = x2d.shape
    if per_row:
        mod_spec = lambda c: pl.BlockSpec((tm, d), lambda i: (i, c))
    else:
        per = rows_per_seq // tm
        mod_spec = lambda c: pl.BlockSpec((None, 1, d), lambda i: (i // per, 0, c))
    return pl.pallas_call(
        _post_kernel,
        grid=(t // tm,),
        in_specs=[
            pl.BlockSpec((tm, d), lambda i: (i, 0)),
            pl.BlockSpec((tm, d), lambda i: (i, 0)),
            mod_spec(2), mod_spec(3), mod_spec(4),
            pl.BlockSpec((1, d), lambda i: (0, 0)),
            pl.BlockSpec((d, d), lambda i: (0, 0)),
            pl.BlockSpec((d, N_EXPERTS), lambda i: (0, 0)),
            pl.BlockSpec((1, N_EXPERTS), lambda i: (0, 0)),
        ],
        out_specs=[
            pl.BlockSpec((tm, d), lambda i: (i, 0)),
            pl.BlockSpec((tm, d), lambda i: (i, 0)),
            pl.BlockSpec((tm, TOP_K), lambda i: (i, 0)),
            pl.BlockSpec((tm, TOP_K), lambda i: (i, 0)),
        ],
        out_shape=[
            jax.ShapeDtypeStruct((t, d), F32),
            jax.ShapeDtypeStruct((t, d), F32),
            jax.ShapeDtypeStruct((t, TOP_K), I32),
            jax.ShapeDtypeStruct((t, TOP_K), F32),
        ],
        compiler_params=_params(("arbitrary",)),
        name="post_mixer",
    )(x2d, o2d, mod, mod, mod, norm_g, w_o, router_w, router_b)


def _dot3_lhs(x, m_bf16):
    hi, mid, lo = _split3(x)
    return _dot(hi, m_bf16) + _dot(mid, m_bf16) + _dot(lo, m_bf16)


def _route_kernel(ti_ref, dest_ref, be_ref, nu_ref, *, n_tok_blocks, tb, n_row_blocks):
    lanes = lax.broadcasted_iota(I32, (tb, N_EXPERTS), 1)

    def onehots(i):
        r0 = pl.multiple_of(i * tb, tb)
        ti = ti_ref[pl.ds(r0, tb), :]
        return [(ti[:, k:k + 1] == lanes).astype(F32) for k in range(TOP_K)]

    def count(i, acc):
        oh = onehots(i)
        return acc + jnp.sum(oh[0] + oh[1] + oh[2] + oh[3], axis=0, keepdims=True)

    counts = lax.fori_loop(0, n_tok_blocks, count, jnp.zeros((1, N_EXPERTS), F32))
    blocks = jnp.floor((counts + (EXPERT_BLOCK - 1)) * (1.0 / EXPERT_BLOCK))
    padded = blocks * EXPERT_BLOCK
    e_r = lax.broadcasted_iota(I32, (N_EXPERTS, N_EXPERTS), 0)
    e_c = lax.broadcasted_iota(I32, (N_EXPERTS, N_EXPERTS), 1)
    upper = (e_r <= e_c).astype(BF16)
    padded_end = _dot3_lhs(padded, upper)
    padded_start = padded_end - padded

    t_r = lax.broadcasted_iota(I32, (tb, tb), 0)
    t_c = lax.broadcasted_iota(I32, (tb, tb), 1)
    strict = (t_c < t_r).astype(BF16)

    def place(i, carry):
        oh = onehots(i)
        c = oh[0] + oh[1] + oh[2] + oh[3]
        rank = _dot(strict, c.astype(BF16)) + carry
        cols = [jnp.sum(oh[k] * rank, axis=1, keepdims=True) for k in range(TOP_K)]
        r0 = pl.multiple_of(i * tb, tb)
        dest_ref[pl.ds(r0, tb), :] = jnp.concatenate(cols, axis=1).astype(I32)
        return carry + jnp.sum(c, axis=0, keepdims=True)

    lax.fori_loop(0, n_tok_blocks, place, padded_start)

    starts = (lax.broadcasted_iota(I32, (n_row_blocks, N_EXPERTS), 0) * EXPERT_BLOCK).astype(F32)
    be = jnp.sum((starts >= padded_end).astype(F32), axis=1, keepdims=True)
    be_ref[...] = jnp.minimum(be, N_EXPERTS - 1.0).astype(I32)
    nu_ref[...] = (padded_end[:, N_EXPERTS - 1:N_EXPERTS] * (1.0 / EXPERT_BLOCK)).astype(I32)


def _route(top_idx, n_row_blocks):
    t = top_idx.shape[0]
    tb = 128
    nbp = -(-n_row_blocks // 8) * 8
    kern = functools.partial(_route_kernel, n_tok_blocks=t // tb, tb=tb, n_row_blocks=nbp)
    dest, be, nu = pl.pallas_call(
        kern,
        out_shape=[
            jax.ShapeDtypeStruct((t, TOP_K), I32),
            jax.ShapeDtypeStruct((nbp, 1), I32),
            jax.ShapeDtypeStruct((1, 1), I32),
        ],
        compiler_params=_params(),
        name="route",
    )(top_idx)
    return dest, be[:n_row_blocks, 0], nu[0]


def _pack_row(tok, slot, t_all, slot_bits):
    return ((tok - t_all // 2) << slot_bits) | slot


def _invert_kernel(dest_hbm, fill_hbm, out_hbm, dest_s, rows_s, sems, *, n_slots, lead, t_all, slot_bits):
    load_dest = pltpu.make_async_copy(dest_hbm, dest_s, sems.at[0])
    load_fill = pltpu.make_async_copy(fill_hbm, rows_s, sems.at[1])
    load_dest.start()
    load_fill.start()
    load_dest.wait()
    load_fill.wait()

    def body(s, carry):
        t = s >> 2
        rows_s[dest_s[s] + lead] = _pack_row(t, (s & (TOP_K - 1)) * t_all + t, t_all, slot_bits)
        return carry

    lax.fori_loop(0, n_slots, body, 0, unroll=8)
    store = pltpu.make_async_copy(rows_s, out_hbm, sems.at[0])
    store.start()
    store.wait()


def _invert(dest_pad, fill, n_slots, lead, t_all, slot_bits):
    kern = functools.partial(_invert_kernel, n_slots=n_slots, lead=lead, t_all=t_all, slot_bits=slot_bits)
    return pl.pallas_call(
        kern,
        in_specs=[pl.BlockSpec(memory_space=pl.ANY), pl.BlockSpec(memory_space=pl.ANY)],
        out_specs=pl.BlockSpec(memory_space=pl.ANY),
        out_shape=jax.ShapeDtypeStruct(fill.shape, I32),
        scratch_shapes=[
            pltpu.SMEM(dest_pad.shape, I32),
            pltpu.SMEM(fill.shape, I32),
            pltpu.SemaphoreType.DMA((2,)),
        ],
        compiler_params=_params(),
        name="invert",
    )(dest_pad, fill)


def _moe_kernel(rows_ref, be_ref, nu_ref, h_hbm, wu_ref, bu_ref, wd_ref, bd_ref, y_hbm,
                x0, x1, y0, y1, wu_bf, wd_bf, gsem, ssem, *, t_all, slot_bits):
    i = pl.program_id(0)
    nu = nu_ref[0]
    xs, ys = (x0, x1), (y0, y1)
    slot_mask = (1 << slot_bits) - 1

    def gather_row(r, s, slot):
        tok = (s >> slot_bits) + t_all // 2
        return pltpu.make_async_copy(h_hbm.at[pl.ds(tok, 1)], xs[slot].at[pl.ds(r, 1)], gsem.at[slot])

    def scatter_row(r, s, slot):
        return pltpu.make_async_copy(ys[slot].at[pl.ds(r, 1)], y_hbm.at[pl.ds(s & slot_mask, 1)], ssem.at[slot])

    def wait_gathers(slot):
        pltpu.make_async_copy(h_hbm.at[pl.ds(0, EXPERT_BLOCK)], xs[slot], gsem.at[slot]).wait()

    def wait_scatters(slot):
        pltpu.make_async_copy(ys[slot], y_hbm.at[pl.ds(0, EXPERT_BLOCK)], ssem.at[slot]).wait()

    @pl.when(i == 0)
    def _():
        y1[...] = jnp.zeros_like(y1)

        def first(r, carry):
            gather_row(r, rows_ref[EXPERT_BLOCK + r], 0).start()
            return carry

        lax.fori_loop(0, EXPERT_BLOCK, first, 0)

    @pl.when(i < nu)
    def _():
        e = be_ref[i]
        prev = be_ref[jnp.maximum(i - 1, 0)]

        @pl.when((i == 0) | (e != prev))
        def _():
            wu_bf[...] = wu_ref[...].astype(BF16)
            wd_bf[...] = wd_ref[...].astype(BF16)

    def block_step(cur):
        nxt = 1 - cur
        wait_gathers(cur)

        @pl.when(i >= 1)
        def _():
            wait_scatters(cur)

        g_base = (i + 2) * EXPERT_BLOCK
        s_base = i * EXPERT_BLOCK
        for r in range(EXPERT_BLOCK):
            gather_row(r, rows_ref[g_base + r], nxt).start()
            scatter_row(r, rows_ref[s_base + r], nxt).start()
        u = _dot(xs[cur][...].astype(BF16), wu_bf[...]) + bu_ref[...]
        glu = jnp.minimum(u[:, :D_FF], SWIGLU_LIMIT)
        lin = jnp.clip(u[:, D_FF:], -SWIGLU_LIMIT, SWIGLU_LIMIT)
        act = glu * _sigmoid(SWIGLU_ALPHA * glu) * (lin + 1.0)
        ys[cur][...] = _dot(act.astype(BF16), wd_bf[...]) + bd_ref[...]

    def drain(cur):
        nxt = 1 - cur
        wait_gathers(cur)
        wait_scatters(cur)

        def last(r, carry):
            scatter_row(r, rows_ref[i * EXPERT_BLOCK + r], nxt).start()
            return carry

        lax.fori_loop(0, EXPERT_BLOCK, last, 0)
        wait_scatters(nxt)

    for parity in (0, 1):
        pl.when((i < nu) & (lax.rem(i, 2) == parity))(functools.partial(block_step, parity))
        pl.when((i == nu) & (lax.rem(i, 2) == parity))(functools.partial(drain, parity))


def _moe(rows_map, block_e, n_used, h2, layer, w_up, b_up, w_down, b_down, n_steps, y_rows, slot_bits):
    t_all, d = h2.shape
    kern = functools.partial(_moe_kernel, t_all=t_all, slot_bits=slot_bits)

    def w_spec(shape):
        return pl.BlockSpec((None, None) + shape, lambda i, rm, be, nu: (layer, be[i], 0, 0))

    return pl.pallas_call(
        kern,
        grid_spec=pltpu.PrefetchScalarGridSpec(
            num_scalar_prefetch=3,
            grid=(n_steps,),
            in_specs=[
                pl.BlockSpec(memory_space=pl.ANY),
                w_spec((d, 2 * D_FF)), w_spec((1, 2 * D_FF)), w_spec((D_FF, d)), w_spec((1, d)),
            ],
            out_specs=pl.BlockSpec(memory_space=pl.ANY),
            scratch_shapes=[
                pltpu.VMEM((EXPERT_BLOCK, d), F32),
                pltpu.VMEM((EXPERT_BLOCK, d), F32),
                pltpu.VMEM((EXPERT_BLOCK, d), F32),
                pltpu.VMEM((EXPERT_BLOCK, d), F32),
                pltpu.VMEM((d, 2 * D_FF), BF16),
                pltpu.VMEM((D_FF, d), BF16),
                pltpu.SemaphoreType.DMA((2,)),
                pltpu.SemaphoreType.DMA((2,)),
            ],
        ),
        out_shape=jax.ShapeDtypeStruct((y_rows, d), F32),
        compiler_params=_params(("arbitrary",)),
        name="experts",
    )(rows_map, block_e, n_used, h2, w_up, b_up, w_down, b_down)


def _combine_kernel(y0_ref, y1_ref, y2_ref, y3_ref, tg_ref, x1_ref, g2_ref, fg_ref, out_ref, *, final):
    tg = tg_ref[...]
    y = tg[:, 0:1] * y0_ref[...]
    for k, y_ref in enumerate((y1_ref, y2_ref, y3_ref), start=1):
        y = y + tg[:, k:k + 1] * y_ref[...]
    x2 = x1_ref[...] + g2_ref[...] * y
    out_ref[...] = _rms(x2, fg_ref[...]) if final else x2


def _combine(y4, tok0, t_all, gate, x1, mod, per_row, rows_per_seq, final_g, final, tb):
    t, d = x1.shape
    if per_row:
        mod_spec = pl.BlockSpec((tb, d), lambda i: (i, 5))
    else:
        per = rows_per_seq // tb
        mod_spec = pl.BlockSpec((None, 1, d), lambda i: (i // per, 0, 5))

    def y_spec(k):
        off = (k * t_all + tok0) // tb
        return pl.BlockSpec((tb, d), lambda i: (off + i, 0))

    kern = functools.partial(_combine_kernel, final=final)
    return pl.pallas_call(
        kern,
        grid=(t // tb,),
        in_specs=[
            y_spec(0), y_spec(1), y_spec(2), y_spec(3),
            pl.BlockSpec((tb, TOP_K), lambda i: (i, 0)),
            pl.BlockSpec((tb, d), lambda i: (i, 0)),
            mod_spec,
            pl.BlockSpec((1, d), lambda i: (0, 0)),
        ],
        out_specs=pl.BlockSpec((tb, d), lambda i: (i, 0)),
        out_shape=jax.ShapeDtypeStruct((t, d), F32),
        compiler_params=_params(("arbitrary",)),
        name="combine",
    )(y4, y4, y4, y4, gate, x1, mod, final_g)


def kernel(x_prompt, x_sample, state_hgrn, state_gla, c_prompt, c_sample, hgrn_lower_bounds, norm1_g, norm2_g, w_ada, b_ada, w_in, w_gk2, b_gk2, gn_hgrn, gn_gla, w_o, router_w, router_b, w_up, b_up, w_down, b_down, final_norm_g):
    depth = w_in.shape[0]
    batch, seq, d = x_prompt.shape
    n_s = x_sample.shape[0]
    assert x_sample.shape[1] == 1 and d == D_MODEL
    t_p = batch * seq
    tm_p = min(512, seq)
    tm_s = n_s
    tb = 128
    sb = 16
    assert seq % tm_p == 0 and seq % tb == 0 and n_s % tb == 0 and n_s % sb == 0

    lbs = jnp.cumsum(jax.nn.softmax(hgrn_lower_bounds.astype(F32), axis=0), axis=0)
    lbs = lbs - lbs[0:1]

    mod = _modulation(jnp.concatenate([c_prompt, c_sample], axis=0), w_ada, b_ada)
    w_main = w_in[:, :, :W_MAIN].astype(BF16)
    w_rank = w_in[:, :, W_MAIN:].astype(BF16)
    w_gk2b = w_gk2.astype(BF16)
    w_ob = w_o.astype(BF16)
    b_up4 = b_up.reshape(depth, N_EXPERTS, 1, 2 * D_FF)
    b_down4 = b_down.reshape(depth, N_EXPERTS, 1, d)

    t_all = t_p + n_s
    m_slots = t_all * TOP_K
    n_row_blocks = -(-m_slots // EXPERT_BLOCK) + N_EXPERTS
    y_rows = m_slots + EXPERT_BLOCK
    slot_bits = int(y_rows - 1).bit_length()
    assert t_all % tb == 0 and t_all >= EXPERT_BLOCK and slot_bits + int(t_all - 1).bit_length() <= 32
    map_len = -(-((n_row_blocks + 3) * EXPERT_BLOCK) // SMEM_WORDS) * SMEM_WORDS
    dest_len = -(-m_slots // SMEM_WORDS) * SMEM_WORDS
    filler = jnp.arange(map_len, dtype=I32) % EXPERT_BLOCK
    fill = _pack_row(filler, m_slots + filler, t_all, slot_bits)

    xp = x_prompt.reshape(t_p, d)
    xs = x_sample.reshape(n_s, d)
    hg_p, gl_p, hg_s, gl_s = [], [], [], []
    for l in range(depth):
        mod_p = mod[l, :batch].reshape(batch, 1, 6 * d)
        mod_s = mod[l, batch:]
        lb = lbs[l].reshape(1, -1)
        n1 = norm1_g[l].reshape(1, d)
        n2 = norm2_g[l].reshape(1, d)
        gnh = gn_hgrn[l].reshape(1, -1)
        gng = gn_gla[l].reshape(1, -1)
        bg = b_gk2[l].reshape(1, -1)
        rb = router_b[l].reshape(1, -1)
        final = l == depth - 1
        fg = final_norm_g.reshape(1, d)

        pm_p, v_p = _in_proj(xp, mod_p, False, seq, n1, lb, l, w_main, w_rank, w_gk2b[l], bg, tm_p)
        pm_s, v_s = _in_proj(xs, mod_s, True, 1, n1, lb, l, w_main, w_rank, w_gk2b[l], bg, tm_s)

        o_p, st_p = _prompt_recurrence(pm_p, v_p, gnh, gng, batch, seq, tm_p)
        o_s, sh_s, sg_s = _sample_recurrence(pm_s, v_s, gnh, gng, state_hgrn, state_gla, l, sb)
        sh_p, sg_p = _unpack_prompt_states(st_p)
        hg_p.append(sh_p)
        gl_p.append(sg_p)
        hg_s.append(sh_s)
        gl_s.append(sg_s)

        x1_p, h2_p, ti_p, tg_p = _post(xp, o_p, mod_p, False, seq, n2, w_ob[l], router_w[l], rb, tm_p)
        x1_s, h2_s, ti_s, tg_s = _post(xs, o_s, mod_s, True, 1, n2, w_ob[l], router_w[l], rb, tm_s)

        dest, block_e, n_used = _route(jnp.concatenate([ti_p, ti_s], axis=0), n_row_blocks + 1)
        dest_pad = jnp.pad(dest.reshape(-1), (0, dest_len - m_slots))
        rows_map = _invert(dest_pad, fill, m_slots, EXPERT_BLOCK, t_all, slot_bits)
        h2 = jnp.concatenate([h2_p, h2_s], axis=0)
        y4 = _moe(rows_map, block_e, n_used, h2, l, w_up, b_up4, w_down, b_down4, n_row_blocks + 1, y_rows, slot_bits)
        xp = _combine(y4, 0, t_all, tg_p, x1_p, mod_p, False, seq, fg, final, tb)
        xs = _combine(y4, t_p, t_all, tg_s, x1_s, mod_s, True, 1, fg, final, tb)

    return (xp.reshape(batch, seq, d), xs.reshape(n_s, 1, d),
            jnp.stack(hg_p), jnp.stack(gl_p), jnp.stack(hg_s), jnp.stack(gl_s))
```
